```python
import jax, jax.numpy as jnp
from jax import lax
import numpy as np

D_MODEL = 2048
BATCH = 4
SEQ = 2048
DEPTH = 2

CTX_LEN = 256
GRID_W = 64
HEAD_DIM = 128
D_MIX = D_MODEL
N_NA_HEADS = 12
NA_WIDTH = N_NA_HEADS * HEAD_DIM
N_FOUR_GROUPS = 4
FOUR_DIM = 128
FOUR_WIDTH = N_FOUR_GROUPS * FOUR_DIM
IN_WIDTH = 3 * NA_WIDTH + FOUR_WIDTH
WIN_ROWS = 8
WIN_COLS = 16
D_FF = 5632
N_MOD = 9
EPS = 1e-6

kernel_name = "hybrid_na_fnet_macaron_dit_block"


def _rms(x, w):
    xf = x.astype(jnp.float32)
    y = xf * lax.rsqrt(jnp.mean(xf * xf, axis=-1, keepdims=True) + EPS)
    return (y * w.astype(jnp.float32)).astype(x.dtype)


def _modulate(xn, shift, scale):
    return xn * (1 + scale) + shift


def _swiglu(h, wi, wo):
    g, u = jnp.split(h @ wi, 2, axis=-1)
    return (jax.nn.silu(g) * u) @ wo


def _heads(t, gain=None):
    B, L, _ = t.shape
    t = t.reshape(B, L, N_NA_HEADS, HEAD_DIM)
    return t if gain is None else _rms(t, gain)


def _fourier(f, w_four):
    B, L, _ = f.shape
    fg = f.reshape(B, L, N_FOUR_GROUPS, FOUR_DIM).transpose(0, 2, 1, 3).astype(jnp.float32)
    mixed = jnp.fft.fft2(fg, axes=(-2, -1), norm="ortho").real.astype(f.dtype)
    out = jnp.einsum('bglc,gcd->blgd', mixed, w_four)
    return out.reshape(B, L, FOUR_WIDTH)


def _ctx_attn(qc, kc, vc):
    B, L, H, Dh = qc.shape
    s = jnp.einsum('bqhd,bkhd->bhqk', qc, kc).astype(jnp.float32) * (Dh ** -0.5)
    p = jax.nn.softmax(s, axis=-1).astype(vc.dtype)
    return jnp.einsum('bhqk,bkhd->bqhd', p, vc).reshape(B, L, H * Dh)


def _na_latent(q, k, v, kc, vc, rpb, rows):
    B, N, H, Dh = q.shape
    kh = min(WIN_ROWS, rows)
    kw = WIN_COLS

    def grid(t):
        return t.reshape(B, rows, GRID_W, H, Dh).transpose(0, 3, 1, 2, 4)

    qg, kg, vg = grid(q), grid(k), grid(v)
    r = jnp.arange(rows)
    row_start = jnp.clip(r - kh // 2, 0, rows - kh)
    row_idx = row_start[:, None] + jnp.arange(kh)
    k_rows = jnp.take(kg, row_idx, axis=2)
    v_rows = jnp.take(vg, row_idx, axis=2)

    col = jnp.arange(GRID_W)
    col_start = jnp.clip(col - kw // 2, 0, GRID_W - kw)
    col_in = (col[None, :] >= col_start[:, None]) & (col[None, :] < col_start[:, None] + kw)
    dr = row_idx - r[:, None] + (WIN_ROWS - 1)
    dc = jnp.clip(col[None, :] - col[:, None], -(kw - 1), kw - 1) + (WIN_COLS - 1)
    bias = rpb[:, dr[:, None, :, None], dc[None, :, None, :]]

    scale = Dh ** -0.5
    s_win = jnp.einsum('bhrqd,bhrikd->bhrqik', qg, k_rows).astype(jnp.float32) * scale \
        + bias.astype(jnp.float32)
    s_win = jnp.where(col_in[:, None, :], s_win, -jnp.inf)
    s_ctx = jnp.einsum('bhrqd,bchd->bhrqc', qg, kc).astype(jnp.float32) * scale
    s = jnp.concatenate([s_win.reshape(B, H, rows, GRID_W, kh * GRID_W), s_ctx], axis=-1)
    p = jax.nn.softmax(s, axis=-1).astype(v.dtype)
    p_win = p[..., :kh * GRID_W].reshape(B, H, rows, GRID_W, kh, GRID_W)
    p_ctx = p[..., kh * GRID_W:]
    o = jnp.einsum('bhrqik,bhrikd->bhrqd', p_win, v_rows) \
        + jnp.einsum('bhrqc,bchd->bhrqd', p_ctx, vc)
    return o.transpose(0, 2, 3, 1, 4).reshape(B, N, H * Dh)


def setup_inputs(seed: int = 0) -> dict:
    key = jax.random.key(seed)
    ks = jax.random.split(key, 17)

    def nrm(k, shape, s):
        return jax.random.normal(k, shape, jnp.float32) * s

    return {
        "x": nrm(ks[0], (BATCH, SEQ, D_MODEL), 1.0),
        "c": nrm(ks[1], (BATCH, D_MODEL), 1.0),
        "ctx": nrm(ks[2], (BATCH, CTX_LEN, D_MODEL), 1.0),
        "c_ctx": nrm(ks[3], (D_MODEL,), 1.0),
        "w_mod": nrm(ks[4], (DEPTH, D_MODEL, N_MOD * D_MODEL), 0.5 * D_MODEL ** -0.5),
        "b_mod": nrm(ks[5], (DEPTH, N_MOD * D_MODEL), 0.02),
        "norm_w": 1.0 + nrm(ks[6], (DEPTH, 3, D_MODEL), 0.1),
        "ffn1_wi": nrm(ks[7], (DEPTH, D_MODEL, 2 * D_FF), D_MODEL ** -0.5),
        "ffn1_wo": nrm(ks[8], (DEPTH, D_FF, D_MODEL), D_FF ** -0.5),
        "w_in": nrm(ks[9], (DEPTH, D_MODEL, IN_WIDTH), D_MODEL ** -0.5),
        "q_norm_w": 1.0 + nrm(ks[10], (DEPTH, HEAD_DIM), 0.1),
        "k_norm_w": 1.0 + nrm(ks[11], (DEPTH, HEAD_DIM), 0.1),
        "rpb": nrm(ks[12], (DEPTH, N_NA_HEADS, 2 * WIN_ROWS - 1, 2 * WIN_COLS - 1), 0.5),
        "w_four": nrm(ks[13], (DEPTH, N_FOUR_GROUPS, FOUR_DIM, FOUR_DIM), FOUR_DIM ** -0.5),
        "w_out": nrm(ks[14], (DEPTH, D_MIX, D_MODEL), D_MIX ** -0.5),
        "ffn2_wi": nrm(ks[15], (DEPTH, D_MODEL, 2 * D_FF), D_MODEL ** -0.5),
        "ffn2_wo": nrm(ks[16], (DEPTH, D_FF, D_MODEL), D_FF ** -0.5),
    }


def reference(x, c, ctx, c_ctx, w_mod, b_mod, norm_w, ffn1_wi, ffn1_wo, w_in,
              q_norm_w, k_norm_w, rpb, w_four, w_out, ffn2_wi, ffn2_wo):
    B, N, D = x.shape
    rows = N // GRID_W
    xc = ctx
    for l in range(DEPTH):
        last = l == DEPTH - 1
        m = [t[:, None, :] for t in jnp.split(jax.nn.silu(c) @ w_mod[l] + b_mod[l], N_MOD, axis=-1)]
        mc = jnp.split(jax.nn.silu(c_ctx) @ w_mod[l] + b_mod[l], N_MOD, axis=-1)

        x = x + 0.5 * m[2] * _swiglu(_modulate(_rms(x, norm_w[l, 0]), m[0], m[1]), ffn1_wi[l], ffn1_wo[l])
        xc = xc + 0.5 * mc[2] * _swiglu(_modulate(_rms(xc, norm_w[l, 0]), mc[0], mc[1]), ffn1_wi[l], ffn1_wo[l])

        h = _modulate(_rms(x, norm_w[l, 1]), m[3], m[4]) @ w_in[l]
        q, k, v, f = jnp.split(h, [NA_WIDTH, 2 * NA_WIDTH, 3 * NA_WIDTH], axis=-1)
        hcn = _modulate(_rms(xc, norm_w[l, 1]), mc[3], mc[4])
        if last:
            kc, vc = jnp.split(hcn @ w_in[l][:, NA_WIDTH:3 * NA_WIDTH], 2, axis=-1)
        else:
            qc, kc, vc, fc = jnp.split(hcn @ w_in[l], [NA_WIDTH, 2 * NA_WIDTH, 3 * NA_WIDTH], axis=-1)
        kc_h = _heads(kc, k_norm_w[l])
        vc_h = _heads(vc)

        na = _na_latent(_heads(q, q_norm_w[l]), _heads(k, k_norm_w[l]), _heads(v), kc_h, vc_h, rpb[l], rows)
        y = jnp.concatenate([na, _fourier(f, w_four[l])], axis=-1) @ w_out[l]
        x = x + m[5] * y

        if not last:
            nac = _ctx_attn(_heads(qc, q_norm_w[l]), kc_h, vc_h)
            yc = jnp.concatenate([nac, _fourier(fc, w_four[l])], axis=-1) @ w_out[l]
            xc = xc + mc[5] * yc
            xc = xc + 0.5 * mc[8] * _swiglu(_modulate(_rms(xc, norm_w[l, 2]), mc[6], mc[7]), ffn2_wi[l], ffn2_wo[l])

        x = x + 0.5 * m[8] * _swiglu(_modulate(_rms(x, norm_w[l, 2]), m[6], m[7]), ffn2_wi[l], ffn2_wo[l])
    return x
```

```python
import functools
import math

import jax
import jax.numpy as jnp
from jax import lax
from jax.experimental import pallas as pl
from jax.experimental.pallas import tpu as pltpu

GRID_W = 64
HEAD_DIM = 128
N_NA_HEADS = 12
NA_WIDTH = N_NA_HEADS * HEAD_DIM
N_FOUR_GROUPS = 4
FOUR_DIM = 128
FOUR_WIDTH = N_FOUR_GROUPS * FOUR_DIM
IN_WIDTH = 3 * NA_WIDTH + FOUR_WIDTH
WIN_ROWS = 8
WIN_COLS = 16
N_MOD = 9
EPS = 1e-6

BF16 = jnp.bfloat16
F32 = jnp.float32

V7X_VMEM_LIMIT_BYTES = 56 * 1024 * 1024
MOD_ROWS = 8
N_BIAS_PAIRS = 2 * WIN_ROWS - 2


def _params(*sem):
    return pltpu.CompilerParams(dimension_semantics=sem, vmem_limit_bytes=V7X_VMEM_LIMIT_BYTES)


def _dot(a, b):
    return jnp.dot(a, b, preferred_element_type=F32)


def _dot_nt(a, b):
    return lax.dot_general(a, b, (((1,), (1,)), ((), ())), preferred_element_type=F32)


def _silu(x):
    return x / (1.0 + jnp.exp(-x))


def _rms_modulate(x, nw, shift, scale):
    y = x * lax.rsqrt(jnp.mean(x * x, axis=-1, keepdims=True) + EPS)
    return (y * nw) * (1.0 + scale) + shift


def _mod_body(c_ref, w_ref, b_ref, o_ref):
    s = _silu(c_ref[...]).astype(BF16)
    o_ref[...] = _dot(s, w_ref[...].astype(BF16)) + b_ref[...]


def _mod_call(cvec, w_mod, b_mod, tn=1024):
    depth, d, nout = w_mod.shape
    return pl.pallas_call(
        _mod_body,
        grid=(depth, nout // tn),
        in_specs=[
            pl.BlockSpec((MOD_ROWS, d), lambda l, j: (0, 0)),
            pl.BlockSpec((None, d, tn), lambda l, j: (l, 0, j)),
            pl.BlockSpec((None, 1, tn), lambda l, j: (l, 0, j)),
        ],
        out_specs=pl.BlockSpec((None, MOD_ROWS, tn), lambda l, j: (l, 0, j)),
        out_shape=jax.ShapeDtypeStruct((depth, MOD_ROWS, nout), F32),
        compiler_params=_params("arbitrary", "arbitrary"),
        name="mod",
    )(cvec, w_mod, b_mod.reshape(depth, 1, nout))


def _ffn_body(x_ref, sh_ref, sc_ref, g_ref, nw_ref, wg_ref, wu_ref, wo_ref, o_ref, h_scr, acc_scr):
    f = pl.program_id(2)

    @pl.when(f == 0)
    def _():
        h_scr[...] = _rms_modulate(x_ref[...], nw_ref[...], sh_ref[...], sc_ref[...]).astype(BF16)
        acc_scr[...] = jnp.zeros_like(acc_scr)

    h = h_scr[...]
    g = _dot(h, wg_ref[...])
    u = _dot(h, wu_ref[...])
    a = (_silu(g) * u).astype(BF16)
    acc_scr[...] += _dot(a, wo_ref[...])

    @pl.when(f == pl.num_programs(2) - 1)
    def _():
        o_ref[...] = x_ref[...] + (0.5 * g_ref[...]) * acc_scr[...]


def _ffn_call(x, shift, scale, gate, nw, wi, wo, tm=512, tf=512):
    g_, r_, d = x.shape
    ff = wo.shape[0]
    nf = ff // tf
    vec = pl.BlockSpec((None, 1, d), lambda g, i, f: (g, 0, 0))
    return pl.pallas_call(
        _ffn_body,
        grid=(g_, r_ // tm, nf),
        in_specs=[
            pl.BlockSpec((None, tm, d), lambda g, i, f: (g, i, 0)),
            vec, vec, vec,
            pl.BlockSpec((1, d), lambda g, i, f: (0, 0)),
            pl.BlockSpec((d, tf), lambda g, i, f: (0, f)),
            pl.BlockSpec((d, tf), lambda g, i, f: (0, f + nf)),
            pl.BlockSpec((tf, d), lambda g, i, f: (f, 0)),
        ],
        out_specs=pl.BlockSpec((None, tm, d), lambda g, i, f: (g, i, 0)),
        out_shape=jax.ShapeDtypeStruct(x.shape, F32),
        scratch_shapes=[pltpu.VMEM((tm, d), BF16), pltpu.VMEM((tm, d), F32)],
        compiler_params=_params("arbitrary", "arbitrary", "arbitrary"),
        name="ffn",
    )(x, shift, scale, gate, nw, wi, wi, wo)


def _proj_in_body(x_ref, sh_ref, sc_ref, nw_ref, w_ref, gain_ref, o_ref, h_scr, *, n_norm_tiles):
    n = pl.program_id(2)

    @pl.when(n == 0)
    def _():
        h_scr[...] = _rms_modulate(x_ref[...], nw_ref[...], sh_ref[...], sc_ref[...]).astype(BF16)

    acc = _dot(h_scr[...], w_ref[...])

    @pl.when(n < n_norm_tiles)
    def _():
        for hh in range(acc.shape[1] // HEAD_DIM):
            sl = slice(hh * HEAD_DIM, (hh + 1) * HEAD_DIM)
            t = acc[:, sl]
            y = t * lax.rsqrt(jnp.mean(t * t, axis=-1, keepdims=True) + EPS)
            o_ref[:, sl] = (y * gain_ref[:, sl]).astype(BF16)

    @pl.when(n >= n_norm_tiles)
    def _():
        o_ref[...] = acc.astype(BF16)


def _proj_in_call(x, shift, scale, nw, w_in, gain, tm=512, tn=512):
    g_, r_, d = x.shape
    nout = w_in.shape[1]
    vec = pl.BlockSpec((None, 1, d), lambda g, i, n: (g, 0, 0))
    body = functools.partial(_proj_in_body, n_norm_tiles=2 * NA_WIDTH // tn)
    return pl.pallas_call(
        body,
        grid=(g_, r_ // tm, nout // tn),
        in_specs=[
            pl.BlockSpec((None, tm, d), lambda g, i, n: (g, i, 0)),
            vec, vec,
            pl.BlockSpec((1, d), lambda g, i, n: (0, 0)),
            pl.BlockSpec((d, tn), lambda g, i, n: (0, n)),
            pl.BlockSpec((1, tn), lambda g, i, n: (0, n)),
        ],
        out_specs=pl.BlockSpec((None, tm, tn), lambda g, i, n: (g, i, n)),
        out_shape=jax.ShapeDtypeStruct((g_, r_, nout), BF16),
        scratch_shapes=[pltpu.VMEM((tm, d), BF16)],
        compiler_params=_params("arbitrary", "arbitrary", "arbitrary"),
        name="proj_in",
    )(x, shift, scale, nw, w_in, gain)


def _bias_body(rpb_ref, o_ref):
    h = pl.program_id(0)
    cq = lax.broadcasted_iota(jnp.int32, (GRID_W, 2 * GRID_W), 0)
    lane = lax.broadcasted_iota(jnp.int32, (GRID_W, 2 * GRID_W), 1)
    ck = lane & (GRID_W - 1)
    hi = lane >= GRID_W
    dc = jnp.clip(ck - cq, -(WIN_COLS - 1), WIN_COLS - 1) + (WIN_COLS - 1)
    col_start = jnp.clip(cq - WIN_COLS // 2, 0, GRID_W - WIN_COLS)
    col_in = (ck >= col_start) & (ck < col_start + WIN_COLS)
    for t in range(N_BIAS_PAIRS):
        tile = jnp.zeros((GRID_W, 2 * GRID_W), F32)
        for j in range(2 * WIN_COLS - 1):
            lo = rpb_ref[h, t, j]
            up = rpb_ref[h, t + 1, j]
            tile = jnp.where(dc == j, jnp.where(hi, up, lo), tile)
        o_ref[t] = jnp.where(col_in, tile, -jnp.inf)


def _bias_call(rpb):
    heads = rpb.shape[0]
    return pl.pallas_call(
        _bias_body,
        grid=(heads,),
        in_specs=[pl.BlockSpec(memory_space=pltpu.SMEM)],
        out_specs=pl.BlockSpec((None, N_BIAS_PAIRS, GRID_W, 2 * GRID_W), lambda h: (h, 0, 0, 0)),
        out_shape=jax.ShapeDtypeStruct((heads, N_BIAS_PAIRS, GRID_W, 2 * GRID_W), F32),
        compiler_params=_params("arbitrary"),
        name="bias",
    )(rpb)


def _na_body(q_ref, k_ref, v_ref, kc_ref, vc_ref, bt_ref, o_ref, *, rows, kh):
    kc = kc_ref[...]
    vc = vc_ref[...]
    win = kh * GRID_W

    def row(r, carry):
        rs = jnp.clip(r - kh // 2, 0, rows - kh)
        q0 = pl.multiple_of(r * GRID_W, GRID_W)
        k0 = pl.multiple_of(rs * GRID_W, GRID_W)
        q = q_ref[pl.ds(q0, GRID_W), :]
        kw = k_ref[pl.ds(k0, win), :]
        vw = v_ref[pl.ds(k0, win), :]
        s_c = _dot_nt(q, kc)
        s_w = _dot_nt(q, kw)
        t0 = rs - r + (WIN_ROWS - 1)
        parts = []
        for j in range(kh // 2):
            parts.append(s_w[:, j * 2 * GRID_W:(j + 1) * 2 * GRID_W] + bt_ref[t0 + 2 * j])
        m = jnp.max(s_c, axis=-1, keepdims=True)
        for p in parts:
            m = jnp.maximum(m, jnp.max(p, axis=-1, keepdims=True))
        e_c = jnp.exp(s_c - m)
        l = jnp.sum(e_c, axis=-1, keepdims=True)
        o = _dot(e_c.astype(BF16), vc)
        for j, p in enumerate(parts):
            e = jnp.exp(p - m)
            l = l + jnp.sum(e, axis=-1, keepdims=True)
            o = o + _dot(e.astype(BF16), vw[j * 2 * GRID_W:(j + 1) * 2 * GRID_W, :])
        o_ref[pl.ds(q0, GRID_W), :] = (o / l).astype(BF16)
        return carry

    lax.fori_loop(0, rows, row, 0)


def _na_call(hl, hc, bt):
    b_, n_, _ = hl.shape
    lc = hc.shape[1]
    rows = n_ // GRID_W
    kh = min(WIN_ROWS, rows)
    assert kh % 2 == 0
    kofs = NA_WIDTH // HEAD_DIM
    body = functools.partial(_na_body, rows=rows, kh=kh)
    return pl.pallas_call(
        body,
        grid=(b_, N_NA_HEADS),
        in_specs=[
            pl.BlockSpec((None, n_, HEAD_DIM), lambda b, h: (b, 0, h)),
            pl.BlockSpec((None, n_, HEAD_DIM), lambda b, h: (b, 0, kofs + h)),
            pl.BlockSpec((None, n_, HEAD_DIM), lambda b, h: (b, 0, 2 * kofs + h)),
            pl.BlockSpec((None, lc, HEAD_DIM), lambda b, h: (b, 0, kofs + h)),
            pl.BlockSpec((None, lc, HEAD_DIM), lambda b, h: (b, 0, 2 * kofs + h)),
            pl.BlockSpec((None, N_BIAS_PAIRS, GRID_W, 2 * GRID_W), lambda b, h: (h, 0, 0, 0)),
        ],
        out_specs=pl.BlockSpec((None, n_, HEAD_DIM), lambda b, h: (b, 0, h)),
        out_shape=jax.ShapeDtypeStruct((b_, n_, NA_WIDTH), BF16),
        compiler_params=_params("arbitrary", "arbitrary"),
        name="na",
    )(hl, hl, hl, hc, hc, bt)


def _ctx_attn_body(q_ref, k_ref, v_ref, o_ref):
    s = _dot_nt(q_ref[...], k_ref[...])
    m = jnp.max(s, axis=-1, keepdims=True)
    e = jnp.exp(s - m)
    l = jnp.sum(e, axis=-1, keepdims=True)
    o_ref[...] = (_dot(e.astype(BF16), v_ref[...]) / l).astype(BF16)


def _ctx_attn_call(hc):
    b_, lc, _ = hc.shape
    kofs = NA_WIDTH // HEAD_DIM
    return pl.pallas_call(
        _ctx_attn_body,
        grid=(b_, N_NA_HEADS),
        in_specs=[
            pl.BlockSpec((None, lc, HEAD_DIM), lambda b, h: (b, 0, h)),
            pl.BlockSpec((None, lc, HEAD_DIM), lambda b, h: (b, 0, kofs + h)),
            pl.BlockSpec((None, lc, HEAD_DIM), lambda b, h: (b, 0, 2 * kofs + h)),
        ],
        out_specs=pl.BlockSpec((None, lc, HEAD_DIM), lambda b, h: (b, 0, h)),
        out_shape=jax.ShapeDtypeStruct((b_, lc, NA_WIDTH), BF16),
        compiler_params=_params("arbitrary", "arbitrary"),
        name="ctx_attn",
    )(hc, hc, hc)


def _dft_tables(length):
    def cos_sin(n):
        k = lax.broadcasted_iota(jnp.int32, (n, n), 0)
        j = lax.broadcasted_iota(jnp.int32, (n, n), 1)
        ang = ((k * j) % n).astype(F32) * (2.0 * math.pi / n)
        return jnp.cos(ang), jnp.sin(ang)

    cl, sl = cos_sin(length)
    cc, sc = cos_sin(FOUR_DIM)
    scale = 1.0 / math.sqrt(length * FOUR_DIM)
    chan = jnp.concatenate([cc * scale, sc * scale], axis=1).astype(BF16)
    return cl.astype(BF16), (-sl).astype(BF16), chan


def _fourier_body(f_ref, chan_ref, cl_ref, sl_ref, wf_ref, o_ref, zc_scr, zs_scr):
    for g in range(N_FOUR_GROUPS):
        sl = slice(g * FOUR_DIM, (g + 1) * FOUR_DIM)
        z = _dot(f_ref[:, sl], chan_ref[...])
        zc_scr[:, sl] = z[:, :FOUR_DIM].astype(BF16)
        zs_scr[:, sl] = z[:, FOUR_DIM:].astype(BF16)
    mixed = _dot(cl_ref[...], zc_scr[...]) + _dot(sl_ref[...], zs_scr[...])
    for g in range(N_FOUR_GROUPS):
        sl = slice(g * FOUR_DIM, (g + 1) * FOUR_DIM)
        o_ref[:, sl] = _dot(mixed[:, sl].astype(BF16), wf_ref[g]).astype(BF16)


def _fourier_call(h, tables, w_four):
    b_, length, _ = h.shape
    cl, sl, chan = tables
    fblk = 3 * NA_WIDTH // FOUR_WIDTH
    return pl.pallas_call(
        _fourier_body,
        grid=(b_,),
        in_specs=[
            pl.BlockSpec((None, length, FOUR_WIDTH), lambda b: (b, 0, fblk)),
            pl.BlockSpec((FOUR_DIM, 2 * FOUR_DIM), lambda b: (0, 0)),
            pl.BlockSpec((length, length), lambda b: (0, 0)),
            pl.BlockSpec((length, length), lambda b: (0, 0)),
            pl.BlockSpec((N_FOUR_GROUPS, FOUR_DIM, FOUR_DIM), lambda b: (0, 0, 0)),
        ],
        out_specs=pl.BlockSpec((None, length, FOUR_WIDTH), lambda b: (b, 0, 0)),
        out_shape=jax.ShapeDtypeStruct((b_, length, FOUR_WIDTH), BF16),
        scratch_shapes=[pltpu.VMEM((length, FOUR_WIDTH), BF16), pltpu.VMEM((length, FOUR_WIDTH), BF16)],
        compiler_params=_params("arbitrary"),
        name="fourier",
    )(h, chan, cl, sl, w_four)


def _proj_out_body(x_ref, g_ref, na_ref, fo_ref, wa_ref, wf_ref, o_ref):
    y = _dot(na_ref[...], wa_ref[...]) + _dot(fo_ref[...], wf_ref[...])
    o_ref[...] = x_ref[...] + g_ref[...] * y


def _proj_out_call(x, gate, na, fo, w_na, w_fo, tm=512):
    g_, r_, d = x.shape
    return pl.pallas_call(
        _proj_out_body,
        grid=(g_, r_ // tm),
        in_specs=[
            pl.BlockSpec((None, tm, d), lambda g, i: (g, i, 0)),
            pl.BlockSpec((None, 1, d), lambda g, i: (g, 0, 0)),
            pl.BlockSpec((None, tm, NA_WIDTH), lambda g, i: (g, i, 0)),
            pl.BlockSpec((None, tm, FOUR_WIDTH), lambda g, i: (g, i, 0)),
            pl.BlockSpec((NA_WIDTH, d), lambda g, i: (0, 0)),
            pl.BlockSpec((FOUR_WIDTH, d), lambda g, i: (0, 0)),
        ],
        out_specs=pl.BlockSpec((None, tm, d), lambda g, i: (g, i, 0)),
        out_shape=jax.ShapeDtypeStruct(x.shape, F32),
        compiler_params=_params("arbitrary", "arbitrary"),
        name="proj_out",
    )(x, gate, na, fo, w_na, w_fo)


def kernel(x, c, ctx, c_ctx, w_mod, b_mod, norm_w, ffn1_wi, ffn1_wo, w_in, q_norm_w, k_norm_w, rpb,
           w_four, w_out, ffn2_wi, ffn2_wo):
    b_, n_, d = x.shape
    lc = ctx.shape[1]
    depth = w_mod.shape[0]

    cvec = jnp.zeros((MOD_ROWS, d), F32).at[:b_].set(c).at[b_].set(c_ctx)
    mod = _mod_call(cvec, w_mod, b_mod).reshape(depth, MOD_ROWS, N_MOD, d)

    tables_lat = _dft_tables(n_)
    tables_ctx = _dft_tables(lc)
    ones = jnp.ones((NA_WIDTH + FOUR_WIDTH,), F32)

    xc = ctx.reshape(1, b_ * lc, d)
    for l in range(depth):
        last = l == depth - 1
        m = [mod[l, :b_, i].reshape(b_, 1, d) for i in range(N_MOD)]
        mc = [mod[l, b_, i].reshape(1, 1, d) for i in range(N_MOD)]
        nw = [norm_w[l, i].reshape(1, d) for i in range(3)]
        wi1, wo1 = ffn1_wi[l].astype(BF16), ffn1_wo[l].astype(BF16)
        wi2, wo2 = ffn2_wi[l].astype(BF16), ffn2_wo[l].astype(BF16)
        win = w_in[l].astype(BF16)
        w_na, w_fo = w_out[l, :NA_WIDTH].astype(BF16), w_out[l, NA_WIDTH:].astype(BF16)
        wf = w_four[l].astype(BF16)
        gain = jnp.concatenate([
            jnp.tile(q_norm_w[l] * HEAD_DIM ** -0.5, N_NA_HEADS),
            jnp.tile(k_norm_w[l], N_NA_HEADS), ones]).reshape(1, IN_WIDTH)

        x = _ffn_call(x, m[0], m[1], m[2], nw[0], wi1, wo1)
        xc = _ffn_call(xc, mc[0], mc[1], mc[2], nw[0], wi1, wo1)

        hl = _proj_in_call(x, m[3], m[4], nw[1], win, gain)
        hc = _proj_in_call(xc, mc[3], mc[4], nw[1], win, gain).reshape(b_, lc, IN_WIDTH)

        bt = _bias_call(rpb[l])
        na = _na_call(hl, hc, bt)
        fo = _fourier_call(hl, tables_lat, wf)
        x = _proj_out_call(x, m[5], na, fo, w_na, w_fo)

        if not last:
            nac = _ctx_attn_call(hc).reshape(1, b_ * lc, NA_WIDTH)
            foc = _fourier_call(hc, tables_ctx, wf).reshape(1, b_ * lc, FOUR_WIDTH)
            xc = _proj_out_call(xc, mc[5], nac, foc, w_na, w_fo)
            xc = _ffn_call(xc, mc[6], mc[7], mc[8], nw[2], wi2, wo2)

        x = _ffn_call(x, m[6], m[7], m[8], nw[2], wi2, wo2)
    return x
```

```python
import functools
import math

import jax
import jax.numpy as jnp
from jax import lax
from jax.experimental import pallas as pl
from jax.experimental.pallas import tpu as pltpu

GRID_W = 64
HEAD_DIM = 128
N_NA_HEADS = 12
NA_WIDTH = N_NA_HEADS * HEAD_DIM
N_FOUR_GROUPS = 4
FOUR_DIM = 128
FOUR_WIDTH = N_FOUR_GROUPS * FOUR_DIM
IN_WIDTH = 3 * NA_WIDTH + FOUR_WIDTH
WIN_ROWS = 8
WIN_COLS = 16
N_MOD = 9
EPS = 1e-6

BF16 = jnp.bfloat16
F32 = jnp.float32

V7X_VMEM_LIMIT_BYTES = 56 * 1024 * 1024
MOD_ROWS = 8
NA_GROUP_ROWS = 4
NA_UNION_ROWS = 12


def _params(*sem):
    return pltpu.CompilerParams(dimension_semantics=sem, vmem_limit_bytes=V7X_VMEM_LIMIT_BYTES)


def _dot(a, b):
    return jnp.dot(a, b, preferred_element_type=F32)


def _dot_nt(a, b):
    return lax.dot_general(a, b, (((1,), (1,)), ((), ())), preferred_element_type=F32)


def _silu(x):
    return x / (1.0 + jnp.exp(-x))


def _rms_modulate(x, nw, shift, scale):
    y = x * lax.rsqrt(jnp.mean(x * x, axis=-1, keepdims=True) + EPS)
    return (y * nw) * (1.0 + scale) + shift


def _mod_body(c_ref, w_ref, b_ref, o_ref):
    s = _silu(c_ref[...]).astype(BF16)
    o_ref[...] = _dot(s, w_ref[...].astype(BF16)) + b_ref[...]


def _mod_call(cvec, w_mod, b_mod, tn=1024):
    depth, d, nout = w_mod.shape
    return pl.pallas_call(
        _mod_body,
        grid=(depth, nout // tn),
        in_specs=[
            pl.BlockSpec((MOD_ROWS, d), lambda l, j: (0, 0)),
            pl.BlockSpec((None, d, tn), lambda l, j: (l, 0, j)),
            pl.BlockSpec((None, 1, tn), lambda l, j: (l, 0, j)),
        ],
        out_specs=pl.BlockSpec((None, MOD_ROWS, tn), lambda l, j: (l, 0, j)),
        out_shape=jax.ShapeDtypeStruct((depth, MOD_ROWS, nout), F32),
        compiler_params=_params("arbitrary", "arbitrary"),
        name="mod",
    )(cvec, w_mod, b_mod.reshape(depth, 1, nout))


def _ffn_body(x_ref, sh_ref, sc_ref, g_ref, nw_ref, wg_ref, wu_ref, wo_ref, o_ref, h_scr):
    f = pl.program_id(2)

    @pl.when(f == 0)
    def _():
        x = x_ref[...]
        h_scr[...] = _rms_modulate(x, nw_ref[...], sh_ref[...], sc_ref[...]).astype(BF16)
        o_ref[...] = x

    h = h_scr[...]
    g = _dot(h, wg_ref[...].astype(BF16))
    u = _dot(h, wu_ref[...].astype(BF16))
    a = (_silu(g) * u).astype(BF16)
    o_ref[...] += (0.5 * g_ref[...]) * _dot(a, wo_ref[...].astype(BF16))


def _ffn_call(x, shift, scale, gate, nw, wi, wo, layer, tm=1024, tf=256):
    g_, r_, d = x.shape
    ff = wo.shape[1]
    nf = ff // tf
    vec = pl.BlockSpec((None, 1, d), lambda g, i, f: (g, 0, 0))
    return pl.pallas_call(
        _ffn_body,
        grid=(g_, r_ // tm, nf),
        in_specs=[
            pl.BlockSpec((None, tm, d), lambda g, i, f: (g, i, 0), pipeline_mode=pl.Buffered(1)),
            vec, vec, vec,
            pl.BlockSpec((1, d), lambda g, i, f: (0, 0)),
            pl.BlockSpec((None, d, tf), lambda g, i, f: (layer, 0, f)),
            pl.BlockSpec((None, d, tf), lambda g, i, f: (layer, 0, f + nf)),
            pl.BlockSpec((None, tf, d), lambda g, i, f: (layer, f, 0)),
        ],
        out_specs=pl.BlockSpec((None, tm, d), lambda g, i, f: (g, i, 0)),
        out_shape=jax.ShapeDtypeStruct(x.shape, F32),
        scratch_shapes=[pltpu.VMEM((tm, d), BF16)],
        compiler_params=_params("arbitrary", "arbitrary", "arbitrary"),
        name="ffn",
    )(x, shift, scale, gate, nw, wi, wi, wo)


def _proj_in_body(x_ref, sh_ref, sc_ref, nw_ref, w_ref, gain_ref, o_ref, h_scr, *, n_norm_tiles):
    n = pl.program_id(2)

    @pl.when(n == 0)
    def _():
        h_scr[...] = _rms_modulate(x_ref[...], nw_ref[...], sh_ref[...], sc_ref[...]).astype(BF16)

    acc = _dot(h_scr[...], w_ref[...].astype(BF16))

    @pl.when(n < n_norm_tiles)
    def _():
        for hh in range(acc.shape[1] // HEAD_DIM):
            sl = slice(hh * HEAD_DIM, (hh + 1) * HEAD_DIM)
            t = acc[:, sl]
            y = t * lax.rsqrt(jnp.mean(t * t, axis=-1, keepdims=True) + EPS)
            o_ref[:, sl] = (y * gain_ref[:, sl]).astype(BF16)

    @pl.when(n >= n_norm_tiles)
    def _():
        o_ref[...] = acc.astype(BF16)


def _proj_in_call(x, shift, scale, nw, w_in, gain, layer, tm=1024, tn=512):
    g_, r_, d = x.shape
    nout = w_in.shape[2]
    vec = pl.BlockSpec((None, 1, d), lambda g, i, n: (g, 0, 0))
    body = functools.partial(_proj_in_body, n_norm_tiles=2 * NA_WIDTH // tn)
    return pl.pallas_call(
        body,
        grid=(g_, r_ // tm, nout // tn),
        in_specs=[
            pl.BlockSpec((None, tm, d), lambda g, i, n: (g, i, 0)),
            vec, vec,
            pl.BlockSpec((1, d), lambda g, i, n: (0, 0)),
            pl.BlockSpec((None, d, tn), lambda g, i, n: (layer, 0, n)),
            pl.BlockSpec((1, tn), lambda g, i, n: (0, n)),
        ],
        out_specs=pl.BlockSpec((None, tm, tn), lambda g, i, n: (g, i, n)),
        out_shape=jax.ShapeDtypeStruct((g_, r_, nout), BF16),
        scratch_shapes=[pltpu.VMEM((tm, d), BF16)],
        compiler_params=_params("arbitrary", "arbitrary", "arbitrary"),
        name="proj_in",
    )(x, shift, scale, nw, w_in, gain)


def _na_plan(rows, kh):
    assert rows % NA_GROUP_ROWS == 0 and rows >= NA_UNION_ROWS and NA_UNION_ROWS % 2 == 0
    assert NA_UNION_ROWS >= NA_GROUP_ROWS + kh - 1
    groups, entries = [], {}
    for r0 in range(0, rows, NA_GROUP_ROWS):
        u0 = min(max(r0 - kh // 2, 0), rows - NA_UNION_ROWS)
        keys = []
        for r in range(r0, r0 + NA_GROUP_ROWS):
            rs = min(max(r - kh // 2, 0), rows - kh)
            assert u0 <= rs and rs + kh <= u0 + NA_UNION_ROWS
            row = []
            for j0 in range(u0, u0 + NA_UNION_ROWS, 2):
                v0, v1 = rs <= j0 < rs + kh, rs <= j0 + 1 < rs + kh
                if v0 and v1:
                    key = ("both", j0 - r + WIN_ROWS - 1)
                elif v0:
                    key = ("lo", j0 - r + WIN_ROWS - 1)
                elif v1:
                    key = ("hi", j0 + 1 - r + WIN_ROWS - 1)
                else:
                    key = None
                if key is not None:
                    entries.setdefault(key, len(entries))
                row.append(key)
            keys.append(row)
        groups.append((r0, u0, keys))
    return groups, entries


def _bias_body(rpb_ref, o_ref, *, entries):
    h = pl.program_id(0)
    shape = (GRID_W, 2 * GRID_W)
    cq = lax.broadcasted_iota(jnp.int32, shape, 0)
    lane = lax.broadcasted_iota(jnp.int32, shape, 1)
    ck = lane & (GRID_W - 1)
    hi = lane >= GRID_W
    dc = jnp.clip(ck - cq, -(WIN_COLS - 1), WIN_COLS - 1) + (WIN_COLS - 1)
    col_start = jnp.clip(cq - WIN_COLS // 2, 0, GRID_W - WIN_COLS)
    col_in = (ck >= col_start) & (ck < col_start + WIN_COLS)
    for (kind, t), slot in entries.items():
        t_lo, t_hi = (t, t + 1) if kind == "both" else (t, t)
        tile = jnp.zeros(shape, F32)
        for j in range(2 * WIN_COLS - 1):
            tile = jnp.where(dc == j, jnp.where(hi, rpb_ref[h, t_hi, j], rpb_ref[h, t_lo, j]), tile)
        valid = col_in if kind == "both" else (col_in & hi if kind == "hi" else col_in & ~hi)
        o_ref[slot] = jnp.where(valid, tile, -jnp.inf)


def _bias_call(rpb, entries):
    nh = rpb.shape[0]
    nslot = len(entries)
    return pl.pallas_call(
        functools.partial(_bias_body, entries=entries),
        grid=(nh,),
        in_specs=[pl.BlockSpec(memory_space=pltpu.SMEM)],
        out_specs=pl.BlockSpec((None, nslot, GRID_W, 2 * GRID_W), lambda h: (h, 0, 0, 0)),
        out_shape=jax.ShapeDtypeStruct((nh, nslot, GRID_W, 2 * GRID_W), F32),
        compiler_params=_params("arbitrary"),
        name="bias",
    )(rpb)


def _na_body(q_ref, k_ref, v_ref, kc_ref, vc_ref, bt_ref, o_ref, p_scr, *, groups, entries):
    kc = kc_ref[...]
    vc = vc_ref[...]
    lc = kc.shape[0]
    pair = 2 * GRID_W
    nwin = NA_UNION_ROWS * GRID_W
    for gi, (r0, u0, keys) in enumerate(groups):
        p_buf = p_scr.at[gi % 2]
        qs = slice(r0 * GRID_W, (r0 + NA_GROUP_ROWS) * GRID_W)
        us = slice(u0 * GRID_W, (u0 + NA_UNION_ROWS) * GRID_W)
        q = q_ref[qs, :]
        s_w = _dot_nt(q, k_ref[us, :])
        s_c = _dot_nt(q, kc)
        inv_l = []
        for i, row in enumerate(keys):
            rsl = slice(i * GRID_W, (i + 1) * GRID_W)
            sc_i = s_c[rsl]
            tiles = {p: s_w[rsl, p * pair:(p + 1) * pair] + bt_ref[entries[key]]
                     for p, key in enumerate(row) if key is not None}
            m_el = sc_i[:, :pair]
            for c0 in range(pair, lc, pair):
                m_el = jnp.maximum(m_el, sc_i[:, c0:c0 + pair])
            for t in tiles.values():
                m_el = jnp.maximum(m_el, t)
            m = jnp.max(m_el, axis=-1, keepdims=True)
            e_c = jnp.exp(sc_i - m)
            p_buf[rsl, nwin:] = e_c.astype(BF16)
            l_el = e_c[:, :pair]
            for c0 in range(pair, lc, pair):
                l_el = l_el + e_c[:, c0:c0 + pair]
            for p in range(len(row)):
                if p in tiles:
                    e = jnp.exp(tiles[p] - m)
                    l_el = l_el + e
                    p_buf[rsl, p * pair:(p + 1) * pair] = e.astype(BF16)
                else:
                    p_buf[rsl, p * pair:(p + 1) * pair] = jnp.zeros((GRID_W, pair), BF16)
            inv_l.append(1.0 / jnp.sum(l_el, axis=-1, keepdims=True))
        o = _dot(p_buf[:, :nwin], v_ref[us, :]) + _dot(p_buf[:, nwin:], vc)
        o_ref[qs, :] = (o * jnp.concatenate(inv_l, axis=0)).astype(BF16)


def _na_call(hl, hc, bt, layer, plan):
    b_, n_, _ = hl.shape
    lc = hc.shape[1]
    groups, entries = plan
    nslot = len(entries)
    kofs = NA_WIDTH // HEAD_DIM
    body = functools.partial(_na_body, groups=groups, entries=entries)
    return pl.pallas_call(
        body,
        grid=(b_, N_NA_HEADS),
        in_specs=[
            pl.BlockSpec((None, n_, HEAD_DIM), lambda b, h: (b, 0, h)),
            pl.BlockSpec((None, n_, HEAD_DIM), lambda b, h: (b, 0, kofs + h)),
            pl.BlockSpec((None, n_, HEAD_DIM), lambda b, h: (b, 0, 2 * kofs + h)),
            pl.BlockSpec((None, lc, HEAD_DIM), lambda b, h: (b, 0, kofs + h)),
            pl.BlockSpec((None, lc, HEAD_DIM), lambda b, h: (b, 0, 2 * kofs + h)),
            pl.BlockSpec((None, nslot, GRID_W, 2 * GRID_W), lambda b, h: (layer * N_NA_HEADS + h, 0, 0, 0)),
        ],
        out_specs=pl.BlockSpec((None, n_, HEAD_DIM), lambda b, h: (b, 0, h)),
        out_shape=jax.ShapeDtypeStruct((b_, n_, NA_WIDTH), BF16),
        scratch_shapes=[pltpu.VMEM((2, NA_GROUP_ROWS * GRID_W, NA_UNION_ROWS * GRID_W + lc), BF16)],
        compiler_params=_params("arbitrary", "arbitrary"),
        name="na",
    )(hl, hl, hl, hc, hc, bt)


def _ctx_attn_body(q_ref, k_ref, v_ref, o_ref):
    s = _dot_nt(q_ref[...], k_ref[...])
    m = jnp.max(s, axis=-1, keepdims=True)
    e = jnp.exp(s - m)
    l = jnp.sum(e, axis=-1, keepdims=True)
    o_ref[...] = (_dot(e.astype(BF16), v_ref[...]) / l).astype(BF16)


def _ctx_attn_call(hc):
    b_, lc, _ = hc.shape
    kofs = NA_WIDTH // HEAD_DIM
    return pl.pallas_call(
        _ctx_attn_body,
        grid=(b_, N_NA_HEADS),
        in_specs=[
            pl.BlockSpec((None, lc, HEAD_DIM), lambda b, h: (b, 0, h)),
            pl.BlockSpec((None, lc, HEAD_DIM), lambda b, h: (b, 0, kofs + h)),
            pl.BlockSpec((None, lc, HEAD_DIM), lambda b, h: (b, 0, 2 * kofs + h)),
        ],
        out_specs=pl.BlockSpec((None, lc, HEAD_DIM), lambda b, h: (b, 0, h)),
        out_shape=jax.ShapeDtypeStruct((b_, lc, NA_WIDTH), BF16),
        compiler_params=_params("arbitrary", "arbitrary"),
        name="ctx_attn",
    )(hc, hc, hc)


def _dft_tables(length):
    def cos_sin(n):
        k = lax.broadcasted_iota(jnp.int32, (n, n), 0)
        j = lax.broadcasted_iota(jnp.int32, (n, n), 1)
        ang = ((k * j) % n).astype(F32) * (2.0 * math.pi / n)
        return jnp.cos(ang), jnp.sin(ang)

    cl, sl = cos_sin(length)
    cc, sc = cos_sin(FOUR_DIM)
    scale = 1.0 / math.sqrt(length * FOUR_DIM)
    chan = jnp.concatenate([cc * scale, sc * scale], axis=1).astype(BF16)
    return cl.astype(BF16), (-sl).astype(BF16), chan


def _fourier_body(f_ref, chan_ref, cl_ref, sl_ref, wf_ref, o_ref, zc_scr, zs_scr):
    for g in range(N_FOUR_GROUPS):
        sl = slice(g * FOUR_DIM, (g + 1) * FOUR_DIM)
        z = _dot(f_ref[:, sl], chan_ref[...])
        zc_scr[:, sl] = z[:, :FOUR_DIM].astype(BF16)
        zs_scr[:, sl] = z[:, FOUR_DIM:].astype(BF16)
    mixed = _dot(cl_ref[...], zc_scr[...]) + _dot(sl_ref[...], zs_scr[...])
    for g in range(N_FOUR_GROUPS):
        sl = slice(g * FOUR_DIM, (g + 1) * FOUR_DIM)
        o_ref[:, sl] = _dot(mixed[:, sl].astype(BF16), wf_ref[g].astype(BF16)).astype(BF16)


def _fourier_call(h, tables, w_four, layer):
    b_, length, _ = h.shape
    cl, sl, chan = tables
    fblk = 3 * NA_WIDTH // FOUR_WIDTH
    return pl.pallas_call(
        _fourier_body,
        grid=(b_,),
        in_specs=[
            pl.BlockSpec((None, length, FOUR_WIDTH), lambda b: (b, 0, fblk)),
            pl.BlockSpec((FOUR_DIM, 2 * FOUR_DIM), lambda b: (0, 0)),
            pl.BlockSpec((length, length), lambda b: (0, 0)),
            pl.BlockSpec((length, length), lambda b: (0, 0)),
            pl.BlockSpec((None, N_FOUR_GROUPS, FOUR_DIM, FOUR_DIM), lambda b: (layer, 0, 0, 0)),
        ],
        out_specs=pl.BlockSpec((None, length, FOUR_WIDTH), lambda b: (b, 0, 0)),
        out_shape=jax.ShapeDtypeStruct((b_, length, FOUR_WIDTH), BF16),
        scratch_shapes=[pltpu.VMEM((length, FOUR_WIDTH), BF16), pltpu.VMEM((length, FOUR_WIDTH), BF16)],
        compiler_params=_params("arbitrary"),
        name="fourier",
    )(h, chan, cl, sl, w_four)


def _proj_out_body(x_ref, g_ref, na_ref, fo_ref, w_ref, o_ref, wb_scr):
    @pl.when((pl.program_id(0) == 0) & (pl.program_id(1) == 0))
    def _():
        wb_scr[...] = w_ref[...].astype(BF16)

    y = _dot(na_ref[...], wb_scr[:NA_WIDTH, :]) + _dot(fo_ref[...], wb_scr[NA_WIDTH:, :])
    o_ref[...] = x_ref[...] + g_ref[...] * y


def _proj_out_call(x, gate, na, fo, w_out, layer, tm=512):
    g_, r_, d = x.shape
    kin = w_out.shape[1]
    return pl.pallas_call(
        _proj_out_body,
        grid=(g_, r_ // tm),
        in_specs=[
            pl.BlockSpec((None, tm, d), lambda g, i: (g, i, 0)),
            pl.BlockSpec((None, 1, d), lambda g, i: (g, 0, 0)),
            pl.BlockSpec((None, tm, NA_WIDTH), lambda g, i: (g, i, 0)),
            pl.BlockSpec((None, tm, FOUR_WIDTH), lambda g, i: (g, i, 0)),
            pl.BlockSpec((None, kin, d), lambda g, i: (layer, 0, 0), pipeline_mode=pl.Buffered(1)),
        ],
        out_specs=pl.BlockSpec((None, tm, d), lambda g, i: (g, i, 0)),
        out_shape=jax.ShapeDtypeStruct(x.shape, F32),
        scratch_shapes=[pltpu.VMEM((kin, d), BF16)],
        compiler_params=_params("arbitrary", "arbitrary"),
        name="proj_out",
    )(x, gate, na, fo, w_out)


def kernel(x, c, ctx, c_ctx, w_mod, b_mod, norm_w, ffn1_wi, ffn1_wo, w_in, q_norm_w, k_norm_w, rpb,
           w_four, w_out, ffn2_wi, ffn2_wo):
    b_, n_, d = x.shape
    lc = ctx.shape[1]
    depth = w_mod.shape[0]
    rows = n_ // GRID_W

    cvec = jnp.zeros((MOD_ROWS, d), F32).at[:b_].set(c).at[b_].set(c_ctx)
    mod = _mod_call(cvec, w_mod, b_mod).reshape(depth, MOD_ROWS, N_MOD, d)

    plan = _na_plan(rows, min(WIN_ROWS, rows))
    bt = _bias_call(rpb.reshape(depth * N_NA_HEADS, 2 * WIN_ROWS - 1, 2 * WIN_COLS - 1), plan[1])
    tables_lat = _dft_tables(n_)
    tables_ctx = _dft_tables(lc)
    ones = jnp.ones((NA_WIDTH + FOUR_WIDTH,), F32)

    xc = ctx.reshape(1, b_ * lc, d)
    for l in range(depth):
        last = l == depth - 1
        m = [mod[l, :b_, i].reshape(b_, 1, d) for i in range(N_MOD)]
        mc = [mod[l, b_, i].reshape(1, 1, d) for i in range(N_MOD)]
        nw = [norm_w[l, i].reshape(1, d) for i in range(3)]
        gain = jnp.concatenate([
            jnp.tile(q_norm_w[l] * HEAD_DIM ** -0.5, N_NA_HEADS),
            jnp.tile(k_norm_w[l], N_NA_HEADS), ones]).reshape(1, IN_WIDTH)

        x = _ffn_call(x, m[0], m[1], m[2], nw[0], ffn1_wi, ffn1_wo, l)
        xc = _ffn_call(xc, mc[0], mc[1], mc[2], nw[0], ffn1_wi, ffn1_wo, l)

        hl = _proj_in_call(x, m[3], m[4], nw[1], w_in, gain, l)
        hc = _proj_in_call(xc, mc[3], mc[4], nw[1], w_in, gain, l).reshape(b_, lc, IN_WIDTH)

        na = _na_call(hl, hc, bt, l, plan)
        fo = _fourier_call(hl, tables_lat, w_four, l)
        x = _proj_out_call(x, m[5], na, fo, w_out, l)

        if not last:
            nac = _ctx_attn_call(hc).reshape(1, b_ * lc, NA_WIDTH)
            foc = _fourier_call(hc, tables_ctx, w_four, l).reshape(1, b_ * lc, FOUR_WIDTH)
            xc = _proj_out_call(xc, mc[5], nac, foc, w_out, l)
            xc = _ffn_call(xc, mc[6], mc[7], mc[8], nw[2], ffn2_wi, ffn2_wo, l)

        x = _ffn_call(x, m[6], m[7], m[8], nw[2], ffn2_wi, ffn2_wo, l)
    return x
```

```python
import functools
import math

import jax
import jax.numpy as jnp
from jax import lax
from jax.experimental import pallas as pl
from jax.experimental.pallas import tpu as pltpu

GRID_W = 64
HEAD_DIM = 128
N_NA_HEADS = 12
NA_WIDTH = N_NA_HEADS * HEAD_DIM
N_FOUR_GROUPS = 4
FOUR_DIM = 128
FOUR_WIDTH = N_FOUR_GROUPS * FOUR_DIM
IN_WIDTH = 3 * NA_WIDTH + FOUR_WIDTH
WIN_ROWS = 8
WIN_COLS = 16
N_MOD = 9
EPS = 1e-6

BF16 = jnp.bfloat16
F32 = jnp.float32

V7X_VMEM_LIMIT_BYTES = 56 * 1024 * 1024
MOD_ROWS = 8
RMS_CHUNK_ROWS = 64
LANES = 128
DFT_ROW_BLOCK = 64
NA_GROUP_ROWS = 4
NA_UNION_ROWS = 12


def _params(*sem):
    return pltpu.CompilerParams(dimension_semantics=sem, vmem_limit_bytes=V7X_VMEM_LIMIT_BYTES)


def _dot(a, b):
    return jnp.dot(a, b, preferred_element_type=F32)


def _dot_nt(a, b):
    return lax.dot_general(a, b, (((1,), (1,)), ((), ())), preferred_element_type=F32)


def _silu(x):
    return x / (1.0 + jnp.exp(-x))


def _rms_modulate_rows(x_ref, nw_ref, sh_ref, sc_ref, h_ref, rs_ref, copy_ref=None):
    n_chunks = x_ref.shape[0] // RMS_CHUNK_ROWS

    def rows_of(c):
        return pl.ds(pl.multiple_of(c * RMS_CHUNK_ROWS, RMS_CHUNK_ROWS), RMS_CHUNK_ROWS)

    def row_scale(c, carry):
        x = x_ref[rows_of(c), :]
        rs = lax.rsqrt(jnp.mean(x * x, axis=-1, keepdims=True) + EPS)
        rs_ref[rows_of(c), :] = jnp.broadcast_to(rs, (RMS_CHUNK_ROWS, rs_ref.shape[1]))
        return carry

    lax.fori_loop(0, n_chunks, row_scale, 0, unroll=4)

    def scale_rows(c, carry):
        rs = rs_ref[rows_of(c), :]
        for c0 in range(0, x_ref.shape[1], LANES):
            cols = slice(c0, c0 + LANES)
            x = x_ref[rows_of(c), cols]
            y = ((x * rs) * nw_ref[:, cols]) * (1.0 + sc_ref[:, cols]) + sh_ref[:, cols]
            h_ref[rows_of(c), cols] = y.astype(BF16)
            if copy_ref is not None:
                copy_ref[rows_of(c), cols] = x
        return carry

    lax.fori_loop(0, n_chunks, scale_rows, 0, unroll=2)


def _mod_body(c_ref, w_ref, b_ref, o_ref):
    s = _silu(c_ref[...]).astype(BF16)
    o_ref[...] = _dot(s, w_ref[...].astype(BF16)) + b_ref[...]


def _mod_call(cvec, w_mod, b_mod, tn=256):
    depth, d, nout = w_mod.shape
    return pl.pallas_call(
        _mod_body,
        grid=(depth, nout // tn),
        in_specs=[
            pl.BlockSpec((MOD_ROWS, d), lambda l, j: (0, 0)),
            pl.BlockSpec((None, d, tn), lambda l, j: (l, 0, j)),
            pl.BlockSpec((None, 1, tn), lambda l, j: (l, 0, j)),
        ],
        out_specs=pl.BlockSpec((None, MOD_ROWS, tn), lambda l, j: (l, 0, j)),
        out_shape=jax.ShapeDtypeStruct((depth, MOD_ROWS, nout), F32),
        compiler_params=_params("arbitrary", "arbitrary"),
        name="mod",
    )(cvec, w_mod, b_mod.reshape(depth, 1, nout))


def _ffn_body(x_ref, sh_ref, sc_ref, g_ref, nw_ref, wg_ref, wu_ref, wo_ref, o_ref, h_scr, rs_scr):
    f = pl.program_id(2)

    @pl.when(f == 0)
    def _():
        _rms_modulate_rows(x_ref, nw_ref, sh_ref, sc_ref, h_scr, rs_scr, copy_ref=o_ref)

    h = h_scr[...]
    g = _dot(h, wg_ref[...].astype(BF16))
    u = _dot(h, wu_ref[...].astype(BF16))
    a = (_silu(g) * u).astype(BF16)
    o_ref[...] += (0.5 * g_ref[...]) * _dot(a, wo_ref[...].astype(BF16))


def _ffn_call(x, shift, scale, gate, nw, wi, wo, layer, tm=1024, tf=256):
    g_, r_, d = x.shape
    ff = wo.shape[1]
    nf = ff // tf
    vec = pl.BlockSpec((None, 1, d), lambda g, i, f: (g, 0, 0))
    return pl.pallas_call(
        _ffn_body,
        grid=(g_, r_ // tm, nf),
        in_specs=[
            pl.BlockSpec((None, tm, d), lambda g, i, f: (g, i, 0)),
            vec, vec, vec,
            pl.BlockSpec((1, d), lambda g, i, f: (0, 0)),
            pl.BlockSpec((None, d, tf), lambda g, i, f: (layer, 0, f)),
            pl.BlockSpec((None, d, tf), lambda g, i, f: (layer, 0, f + nf)),
            pl.BlockSpec((None, tf, d), lambda g, i, f: (layer, f, 0)),
        ],
        out_specs=pl.BlockSpec((None, tm, d), lambda g, i, f: (g, i, 0)),
        out_shape=jax.ShapeDtypeStruct(x.shape, F32),
        scratch_shapes=[pltpu.VMEM((tm, d), BF16), pltpu.VMEM((tm, LANES), F32)],
        compiler_params=_params("arbitrary", "arbitrary", "arbitrary"),
        name="ffn",
    )(x, shift, scale, gate, nw, wi, wi, wo)


def _proj_in_body(x_ref, sh_ref, sc_ref, nw_ref, w_ref, gain_ref, o_ref, h_scr, rs_scr, *, n_norm_tiles):
    n = pl.program_id(2)

    @pl.when(n == 0)
    def _():
        _rms_modulate_rows(x_ref, nw_ref, sh_ref, sc_ref, h_scr, rs_scr)

    acc = _dot(h_scr[...], w_ref[...].astype(BF16))

    @pl.when(n < n_norm_tiles)
    def _():
        for hh in range(acc.shape[1] // HEAD_DIM):
            sl = slice(hh * HEAD_DIM, (hh + 1) * HEAD_DIM)
            t = acc[:, sl]
            y = t * lax.rsqrt(jnp.mean(t * t, axis=-1, keepdims=True) + EPS)
            o_ref[:, sl] = (y * gain_ref[:, sl]).astype(BF16)

    @pl.when(n >= n_norm_tiles)
    def _():
        o_ref[...] = acc.astype(BF16)


def _proj_in_call(x, shift, scale, nw, w_in, gain, layer, tm=1024, tn=512):
    g_, r_, d = x.shape
    nout = w_in.shape[2]
    vec = pl.BlockSpec((None, 1, d), lambda g, i, n: (g, 0, 0))
    body = functools.partial(_proj_in_body, n_norm_tiles=2 * NA_WIDTH // tn)
    return pl.pallas_call(
        body,
        grid=(g_, r_ // tm, nout // tn),
        in_specs=[
            pl.BlockSpec((None, tm, d), lambda g, i, n: (g, i, 0)),
            vec, vec,
            pl.BlockSpec((1, d), lambda g, i, n: (0, 0)),
            pl.BlockSpec((None, d, tn), lambda g, i, n: (layer, 0, n)),
            pl.BlockSpec((1, tn), lambda g, i, n: (0, n)),
        ],
        out_specs=pl.BlockSpec((None, tm, tn), lambda g, i, n: (g, i, n)),
        out_shape=jax.ShapeDtypeStruct((g_, r_, nout), BF16),
        scratch_shapes=[pltpu.VMEM((tm, d), BF16), pltpu.VMEM((tm, LANES), F32)],
        compiler_params=_params("arbitrary", "arbitrary", "arbitrary"),
        name="proj_in",
    )(x, shift, scale, nw, w_in, gain)


def _na_plan(rows, kh):
    assert rows % NA_GROUP_ROWS == 0 and rows >= NA_UNION_ROWS and NA_UNION_ROWS % 2 == 0
    assert NA_UNION_ROWS >= NA_GROUP_ROWS + kh - 1
    groups, entries = [], {}
    for r0 in range(0, rows, NA_GROUP_ROWS):
        u0 = min(max(r0 - kh // 2, 0), rows - NA_UNION_ROWS)
        keys = []
        for r in range(r0, r0 + NA_GROUP_ROWS):
            rs = min(max(r - kh // 2, 0), rows - kh)
            assert u0 <= rs and rs + kh <= u0 + NA_UNION_ROWS
            row = []
            for j0 in range(u0, u0 + NA_UNION_ROWS, 2):
                v0, v1 = rs <= j0 < rs + kh, rs <= j0 + 1 < rs + kh
                if v0 and v1:
                    key = ("both", j0 - r + WIN_ROWS - 1)
                elif v0:
                    key = ("lo", j0 - r + WIN_ROWS - 1)
                elif v1:
                    key = ("hi", j0 + 1 - r + WIN_ROWS - 1)
                else:
                    key = None
                if key is not None:
                    entries.setdefault(key, len(entries))
                row.append(key)
            keys.append(row)
        groups.append((r0, u0, keys))
    return groups, entries


def _bias_body(rpb_ref, o_ref, *, entries):
    h = pl.program_id(0)
    shape = (GRID_W, 2 * GRID_W)
    cq = lax.broadcasted_iota(jnp.int32, shape, 0)
    lane = lax.broadcasted_iota(jnp.int32, shape, 1)
    ck = lane & (GRID_W - 1)
    hi = lane >= GRID_W
    dc = jnp.clip(ck - cq, -(WIN_COLS - 1), WIN_COLS - 1) + (WIN_COLS - 1)
    col_start = jnp.clip(cq - WIN_COLS // 2, 0, GRID_W - WIN_COLS)
    col_in = (ck >= col_start) & (ck < col_start + WIN_COLS)
    for (kind, t), slot in entries.items():
        t_lo, t_hi = (t, t + 1) if kind == "both" else (t, t)
        tile = jnp.zeros(shape, F32)
        for j in range(2 * WIN_COLS - 1):
            tile = jnp.where(dc == j, jnp.where(hi, rpb_ref[h, t_hi, j], rpb_ref[h, t_lo, j]), tile)
        valid = col_in if kind == "both" else (col_in & hi if kind == "hi" else col_in & ~hi)
        o_ref[slot] = jnp.where(valid, tile, -jnp.inf)


def _bias_call(rpb, entries):
    nh = rpb.shape[0]
    nslot = len(entries)
    return pl.pallas_call(
        functools.partial(_bias_body, entries=entries),
        grid=(nh,),
        in_specs=[pl.BlockSpec(memory_space=pltpu.SMEM)],
        out_specs=pl.BlockSpec((None, nslot, GRID_W, 2 * GRID_W), lambda h: (h, 0, 0, 0)),
        out_shape=jax.ShapeDtypeStruct((nh, nslot, GRID_W, 2 * GRID_W), F32),
        compiler_params=_params("arbitrary"),
        name="bias",
    )(rpb)


def _na_body(q_ref, k_ref, v_ref, kc_ref, vc_ref, bt_ref, o_ref, p_scr, *, groups, entries):
    kc = kc_ref[...]
    vc = vc_ref[...]
    lc = kc.shape[0]
    pair = 2 * GRID_W
    nwin = NA_UNION_ROWS * GRID_W
    for gi, (r0, u0, keys) in enumerate(groups):
        p_buf = p_scr.at[gi % 2]
        qs = slice(r0 * GRID_W, (r0 + NA_GROUP_ROWS) * GRID_W)
        us = slice(u0 * GRID_W, (u0 + NA_UNION_ROWS) * GRID_W)
        q = q_ref[qs, :]
        s_w = _dot_nt(q, k_ref[us, :])
        s_c = _dot_nt(q, kc)
        inv_l = []
        for i, row in enumerate(keys):
            rsl = slice(i * GRID_W, (i + 1) * GRID_W)
            sc_i = s_c[rsl]
            tiles = {p: s_w[rsl, p * pair:(p + 1) * pair] + bt_ref[entries[key]]
                     for p, key in enumerate(row) if key is not None}
            m_el = sc_i[:, :pair]
            for c0 in range(pair, lc, pair):
                m_el = jnp.maximum(m_el, sc_i[:, c0:c0 + pair])
            for t in tiles.values():
                m_el = jnp.maximum(m_el, t)
            m = jnp.max(m_el, axis=-1, keepdims=True)
            e_c = jnp.exp(sc_i - m)
            p_buf[rsl, nwin:] = e_c.astype(BF16)
            l_el = e_c[:, :pair]
            for c0 in range(pair, lc, pair):
                l_el = l_el + e_c[:, c0:c0 + pair]
            for p in range(len(row)):
                if p in tiles:
                    e = jnp.exp(tiles[p] - m)
                    l_el = l_el + e
                    p_buf[rsl, p * pair:(p + 1) * pair] = e.astype(BF16)
                else:
                    p_buf[rsl, p * pair:(p + 1) * pair] = jnp.zeros((GRID_W, pair), BF16)
            inv_l.append(1.0 / jnp.sum(l_el, axis=-1, keepdims=True))
        o = _dot(p_buf[:, :nwin], v_ref[us, :]) + _dot(p_buf[:, nwin:], vc)
        o_ref[qs, :] = (o * jnp.concatenate(inv_l, axis=0)).astype(BF16)


def _na_call(hl, hc, bt, layer, plan):
    b_, n_, _ = hl.shape
    lc = hc.shape[1]
    groups, entries = plan
    nslot = len(entries)
    kofs = NA_WIDTH // HEAD_DIM
    body = functools.partial(_na_body, groups=groups, entries=entries)
    return pl.pallas_call(
        body,
        grid=(b_, N_NA_HEADS),
        in_specs=[
            pl.BlockSpec((None, n_, HEAD_DIM), lambda b, h: (b, 0, h)),
            pl.BlockSpec((None, n_, HEAD_DIM), lambda b, h: (b, 0, kofs + h)),
            pl.BlockSpec((None, n_, HEAD_DIM), lambda b, h: (b, 0, 2 * kofs + h)),
            pl.BlockSpec((None, lc, HEAD_DIM), lambda b, h: (b, 0, kofs + h)),
            pl.BlockSpec((None, lc, HEAD_DIM), lambda b, h: (b, 0, 2 * kofs + h)),
            pl.BlockSpec((None, nslot, GRID_W, 2 * GRID_W), lambda b, h: (layer * N_NA_HEADS + h, 0, 0, 0)),
        ],
        out_specs=pl.BlockSpec((None, n_, HEAD_DIM), lambda b, h: (b, 0, h)),
        out_shape=jax.ShapeDtypeStruct((b_, n_, NA_WIDTH), BF16),
        scratch_shapes=[pltpu.VMEM((2, NA_GROUP_ROWS * GRID_W, NA_UNION_ROWS * GRID_W + lc), BF16)],
        compiler_params=_params("arbitrary", "arbitrary"),
        name="na",
    )(hl, hl, hl, hc, hc, bt)


def _ctx_attn_body(q_ref, k_ref, v_ref, o_ref):
    s = _dot_nt(q_ref[...], k_ref[...])
    m = jnp.max(s, axis=-1, keepdims=True)
    e = jnp.exp(s - m)
    l = jnp.sum(e, axis=-1, keepdims=True)
    o_ref[...] = (_dot(e.astype(BF16), v_ref[...]) / l).astype(BF16)


def _ctx_attn_call(hc):
    b_, lc, _ = hc.shape
    kofs = NA_WIDTH // HEAD_DIM
    return pl.pallas_call(
        _ctx_attn_body,
        grid=(b_, N_NA_HEADS),
        in_specs=[
            pl.BlockSpec((None, lc, HEAD_DIM), lambda b, h: (b, 0, h)),
            pl.BlockSpec((None, lc, HEAD_DIM), lambda b, h: (b, 0, kofs + h)),
            pl.BlockSpec((None, lc, HEAD_DIM), lambda b, h: (b, 0, 2 * kofs + h)),
        ],
        out_specs=pl.BlockSpec((None, lc, HEAD_DIM), lambda b, h: (b, 0, h)),
        out_shape=jax.ShapeDtypeStruct((b_, lc, NA_WIDTH), BF16),
        compiler_params=_params("arbitrary", "arbitrary"),
        name="ctx_attn",
    )(hc, hc, hc)


def _dft_tables(length):
    def cos_sin(freq, n_pos, period):
        ang = ((freq[:, None] * jnp.arange(n_pos, dtype=jnp.int32)[None, :]) % period).astype(F32)
        ang = ang * (2.0 * math.pi / period)
        return jnp.cos(ang), jnp.sin(ang)

    assert length % DFT_ROW_BLOCK == 0
    coarse = jnp.arange(length // DFT_ROW_BLOCK, dtype=jnp.int32) * DFT_ROW_BLOCK
    fine = jnp.arange(DFT_ROW_BLOCK, dtype=jnp.int32)
    ca, sa = cos_sin(coarse, length, length)
    cb, sb = cos_sin(fine, length, length)
    cc, sc = cos_sin(jnp.arange(FOUR_DIM, dtype=jnp.int32), FOUR_DIM, FOUR_DIM)
    scale = 1.0 / math.sqrt(length * FOUR_DIM)
    chan = jnp.concatenate([cc * scale, sc * scale], axis=1).astype(BF16)
    return ca, sa, cb, sb, chan


def _fourier_body(f_ref, chan_ref, ca_ref, sa_ref, cb_ref, sb_ref, wf_ref, o_ref, cl_scr, sl_scr, zc_scr, zs_scr):
    @pl.when(pl.program_id(0) == 0)
    def _():
        cb = cb_ref[...]
        sb = sb_ref[...]
        for a in range(ca_ref.shape[0]):
            rows = slice(a * DFT_ROW_BLOCK, (a + 1) * DFT_ROW_BLOCK)
            ca = ca_ref[a:a + 1, :]
            sa = sa_ref[a:a + 1, :]
            cl_scr[rows, :] = (ca * cb - sa * sb).astype(BF16)
            sl_scr[rows, :] = (-(sa * cb + ca * sb)).astype(BF16)

    for g in range(N_FOUR_GROUPS):
        sl = slice(g * FOUR_DIM, (g + 1) * FOUR_DIM)
        z = _dot(f_ref[:, sl], chan_ref[...])
        zc_scr[:, sl] = z[:, :FOUR_DIM].astype(BF16)
        zs_scr[:, sl] = z[:, FOUR_DIM:].astype(BF16)
    mixed = _dot(cl_scr[...], zc_scr[...]) + _dot(sl_scr[...], zs_scr[...])
    for g in range(N_FOUR_GROUPS):
        sl = slice(g * FOUR_DIM, (g + 1) * FOUR_DIM)
        o_ref[:, sl] = _dot(mixed[:, sl].astype(BF16), wf_ref[g].astype(BF16)).astype(BF16)


def _fourier_call(h, tables, w_four, layer):
    b_, length, _ = h.shape
    ca, sa, cb, sb, chan = tables
    fblk = 3 * NA_WIDTH // FOUR_WIDTH
    whole = lambda arr: pl.BlockSpec(arr.shape, lambda b: (0,) * arr.ndim)
    return pl.pallas_call(
        _fourier_body,
        grid=(b_,),
        in_specs=[
            pl.BlockSpec((None, length, FOUR_WIDTH), lambda b: (b, 0, fblk)),
            whole(chan), whole(ca), whole(sa), whole(cb), whole(sb),
            pl.BlockSpec((None, N_FOUR_GROUPS, FOUR_DIM, FOUR_DIM), lambda b: (layer, 0, 0, 0)),
        ],
        out_specs=pl.BlockSpec((None, length, FOUR_WIDTH), lambda b: (b, 0, 0)),
        out_shape=jax.ShapeDtypeStruct((b_, length, FOUR_WIDTH), BF16),
        scratch_shapes=[pltpu.VMEM((length, length), BF16), pltpu.VMEM((length, length), BF16),
                        pltpu.VMEM((length, FOUR_WIDTH), BF16), pltpu.VMEM((length, FOUR_WIDTH), BF16)],
        compiler_params=_params("arbitrary"),
        name="fourier",
    )(h, chan, ca, sa, cb, sb, w_four)


def _proj_out_body(x_ref, g_ref, na_ref, fo_ref, w_ref, o_ref, wb_scr):
    @pl.when((pl.program_id(0) == 0) & (pl.program_id(1) == 0))
    def _():
        wb_scr[...] = w_ref[...].astype(BF16)

    y = _dot(na_ref[...], wb_scr[:NA_WIDTH, :]) + _dot(fo_ref[...], wb_scr[NA_WIDTH:, :])
    o_ref[...] = x_ref[...] + g_ref[...] * y


def _proj_out_call(x, gate, na, fo, w_out, layer, tm=512):
    g_, r_, d = x.shape
    kin = w_out.shape[1]
    return pl.pallas_call(
        _proj_out_body,
        grid=(g_, r_ // tm),
        in_specs=[
            pl.BlockSpec((None, tm, d), lambda g, i: (g, i, 0)),
            pl.BlockSpec((None, 1, d), lambda g, i: (g, 0, 0)),
            pl.BlockSpec((None, tm, NA_WIDTH), lambda g, i: (g, i, 0)),
            pl.BlockSpec((None, tm, FOUR_WIDTH), lambda g, i: (g, i, 0)),
            pl.BlockSpec((None, kin, d), lambda g, i: (layer, 0, 0), pipeline_mode=pl.Buffered(1)),
        ],
        out_specs=pl.BlockSpec((None, tm, d), lambda g, i: (g, i, 0)),
        out_shape=jax.ShapeDtypeStruct(x.shape, F32),
        scratch_shapes=[pltpu.VMEM((kin, d), BF16)],
        compiler_params=_params("arbitrary", "arbitrary"),
        name="proj_out",
    )(x, gate, na, fo, w_out)


def kernel(x, c, ctx, c_ctx, w_mod, b_mod, norm_w, ffn1_wi, ffn1_wo, w_in, q_norm_w, k_norm_w, rpb,
           w_four, w_out, ffn2_wi, ffn2_wo):
    b_, n_, d = x.shape
    lc = ctx.shape[1]
    depth = w_mod.shape[0]
    rows = n_ // GRID_W

    cvec = jnp.zeros((MOD_ROWS, d), F32).at[:b_].set(c).at[b_].set(c_ctx)
    mod = _mod_call(cvec, w_mod, b_mod).reshape(depth, MOD_ROWS, N_MOD, d)

    plan = _na_plan(rows, min(WIN_ROWS, rows))
    bt = _bias_call(rpb.reshape(depth * N_NA_HEADS, 2 * WIN_ROWS - 1, 2 * WIN_COLS - 1), plan[1])
    tables_lat = _dft_tables(n_)
    tables_ctx = _dft_tables(lc)
    ones = jnp.ones((NA_WIDTH + FOUR_WIDTH,), F32)

    xc = ctx.reshape(1, b_ * lc, d)
    for l in range(depth):
        last = l == depth - 1
        m = [mod[l, :b_, i].reshape(b_, 1, d) for i in range(N_MOD)]
        mc = [mod[l, b_, i].reshape(1, 1, d) for i in range(N_MOD)]
        nw = [norm_w[l, i].reshape(1, d) for i in range(3)]
        gain = jnp.concatenate([
            jnp.tile(q_norm_w[l] * HEAD_DIM ** -0.5, N_NA_HEADS),
            jnp.tile(k_norm_w[l], N_NA_HEADS), ones]).reshape(1, IN_WIDTH)

        x = _ffn_call(x, m[0], m[1], m[2], nw[0], ffn1_wi, ffn1_wo, l)
        xc = _ffn_call(xc, mc[0], mc[1], mc[2], nw[0], ffn1_wi, ffn1_wo, l)

        hl = _proj_in_call(x, m[3], m[4], nw[1], w_in, gain, l)
        hc = _proj_in_call(xc, mc[3], mc[4], nw[1], w_in, gain, l).reshape(b_, lc, IN_WIDTH)

        na = _na_call(hl, hc, bt, l, plan)
        fo = _fourier_call(hl, tables_lat, w_four, l)
        x = _proj_out_call(x, m[5], na, fo, w_out, l)

        if not last:
            nac = _ctx_attn_call(hc).reshape(1, b_ * lc, NA_WIDTH)
            foc = _fourier_call(hc, tables_ctx, w_four, l).reshape(1, b_ * lc, FOUR_WIDTH)
            xc = _proj_out_call(xc, mc[5], nac, foc, w_out, l)
            xc = _ffn_call(xc, mc[6], mc[7], mc[8], nw[2], ffn2_wi, ffn2_wo, l)

        x = _ffn_call(x, m[6], m[7], m[8], nw[2], ffn2_wi, ffn2_wo, l)
    return x
```

```python
import functools
import math

import jax
import jax.numpy as jnp
from jax import lax
from jax.experimental import pallas as pl
from jax.experimental.pallas import tpu as pltpu

GRID_W = 64
HEAD_DIM = 128
N_NA_HEADS = 12
NA_WIDTH = N_NA_HEADS * HEAD_DIM
N_FOUR_GROUPS = 4
FOUR_DIM = 128
FOUR_WIDTH = N_FOUR_GROUPS * FOUR_DIM
IN_WIDTH = 3 * NA_WIDTH + FOUR_WIDTH
WIN_ROWS = 8
WIN_COLS = 16
N_MOD = 9
EPS = 1e-6

BF16 = jnp.bfloat16
F32 = jnp.float32

V7X_VMEM_LIMIT_BYTES = 56 * 1024 * 1024
MOD_ROWS = 8
RMS_CHUNK_ROWS = 64
LANES = 128
DFT_ROW_BLOCK = 64
NA_GROUP_ROWS = 4
NA_UNION_ROWS = 12


def _params(*sem):
    return pltpu.CompilerParams(dimension_semantics=sem, vmem_limit_bytes=V7X_VMEM_LIMIT_BYTES)


def _dot(a, b):
    return jnp.dot(a, b, preferred_element_type=F32)


def _dot_nt(a, b):
    return lax.dot_general(a, b, (((1,), (1,)), ((), ())), preferred_element_type=F32)


def _silu(x):
    return x / (1.0 + jnp.exp(-x))


def _rms_modulate_rows(x_ref, nw_ref, sh_ref, sc_ref, h_ref, rs_ref, copy_ref=None):
    n_chunks = x_ref.shape[0] // RMS_CHUNK_ROWS

    def rows_of(c):
        return pl.ds(pl.multiple_of(c * RMS_CHUNK_ROWS, RMS_CHUNK_ROWS), RMS_CHUNK_ROWS)

    def row_scale(c, carry):
        x = x_ref[rows_of(c), :]
        rs = lax.rsqrt(jnp.mean(x * x, axis=-1, keepdims=True) + EPS)
        rs_ref[rows_of(c), :] = jnp.broadcast_to(rs, (RMS_CHUNK_ROWS, rs_ref.shape[1]))
        return carry

    lax.fori_loop(0, n_chunks, row_scale, 0, unroll=4)

    def scale_rows(c, carry):
        rs = rs_ref[rows_of(c), :]
        for c0 in range(0, x_ref.shape[1], LANES):
            cols = slice(c0, c0 + LANES)
            x = x_ref[rows_of(c), cols]
            y = ((x * rs) * nw_ref[:, cols]) * (1.0 + sc_ref[:, cols]) + sh_ref[:, cols]
            h_ref[rows_of(c), cols] = y.astype(BF16)
            if copy_ref is not None:
                copy_ref[rows_of(c), cols] = x
        return carry

    lax.fori_loop(0, n_chunks, scale_rows, 0, unroll=2)


def _mod_body(c_ref, w_ref, b_ref, o_ref):
    s = _silu(c_ref[...]).astype(BF16)
    o_ref[...] = _dot(s, w_ref[...].astype(BF16)) + b_ref[...]


def _mod_call(cvec, w_mod, b_mod, tn=2048):
    depth, d, nout = w_mod.shape
    return pl.pallas_call(
        _mod_body,
        grid=(depth, nout // tn),
        in_specs=[
            pl.BlockSpec((MOD_ROWS, d), lambda l, j: (0, 0)),
            pl.BlockSpec((None, d, tn), lambda l, j: (l, 0, j)),
            pl.BlockSpec((None, 1, tn), lambda l, j: (l, 0, j)),
        ],
        out_specs=pl.BlockSpec((None, MOD_ROWS, tn), lambda l, j: (l, 0, j)),
        out_shape=jax.ShapeDtypeStruct((depth, MOD_ROWS, nout), F32),
        compiler_params=_params("arbitrary", "arbitrary"),
        name="mod",
    )(cvec, w_mod, b_mod.reshape(depth, 1, nout))


def _ffn_body(x_ref, sh_ref, sc_ref, g_ref, nw_ref, wi_hbm, wo_hbm, o_ref,
              h_scr, rs_scr, wg_buf, wu_buf, wo_buf, sem, *, layer, tf, nf, grid):
    tile = pl.program_id(0) * grid[1] + pl.program_id(1)
    n_tiles = grid[0] * grid[1]
    ff = nf * tf

    def weight_copies(f, slot):
        c0 = pl.multiple_of(f * tf, tf)
        c1 = pl.multiple_of(ff + f * tf, tf)
        return (
            pltpu.make_async_copy(wi_hbm.at[layer, :, pl.ds(c0, tf)], wg_buf.at[slot], sem.at[0, slot]),
            pltpu.make_async_copy(wi_hbm.at[layer, :, pl.ds(c1, tf)], wu_buf.at[slot], sem.at[1, slot]),
            pltpu.make_async_copy(wo_hbm.at[layer, pl.ds(c0, tf), :], wo_buf.at[slot], sem.at[2, slot]),
        )

    @pl.when(tile == 0)
    def _():
        for cp in weight_copies(0, 0):
            cp.start()

    _rms_modulate_rows(x_ref, nw_ref, sh_ref, sc_ref, h_scr, rs_scr, copy_ref=o_ref)
    gate = 0.5 * g_ref[...]

    def sub_step(f, slot, f_next, has_next):
        for cp in weight_copies(f, slot):
            cp.wait()

        @pl.when(has_next)
        def _():
            for cp in weight_copies(f_next, 1 - slot):
                cp.start()

        h = h_scr[...]
        g = _dot(h, wg_buf[slot].astype(BF16))
        u = _dot(h, wu_buf[slot].astype(BF16))
        a = (_silu(g) * u).astype(BF16)
        o_ref[...] += gate * _dot(a, wo_buf[slot].astype(BF16))

    def pair(p, carry):
        f0 = 2 * p
        sub_step(f0, 0, f0 + 1, True)
        sub_step(f0 + 1, 1, f0 + 2, True)
        return carry

    lax.fori_loop(0, nf // 2 - 1, pair, 0)
    sub_step(nf - 2, 0, nf - 1, True)
    sub_step(nf - 1, 1, 0, tile + 1 < n_tiles)


def _ffn_call(x, shift, scale, gate, nw, wi, wo, layer, tm=1024, tf=256):
    g_, r_, d = x.shape
    ff = wo.shape[1]
    nf = ff // tf
    assert nf % 2 == 0 and nf * tf == ff
    vec = pl.BlockSpec((None, 1, d), lambda g, i: (g, 0, 0))
    grid = (g_, r_ // tm)
    return pl.pallas_call(
        functools.partial(_ffn_body, layer=layer, tf=tf, nf=nf, grid=grid),
        grid=grid,
        in_specs=[
            pl.BlockSpec((None, tm, d), lambda g, i: (g, i, 0)),
            vec, vec, vec,
            pl.BlockSpec((1, d), lambda g, i: (0, 0)),
            pl.BlockSpec(memory_space=pl.ANY),
            pl.BlockSpec(memory_space=pl.ANY),
        ],
        out_specs=pl.BlockSpec((None, tm, d), lambda g, i: (g, i, 0)),
        out_shape=jax.ShapeDtypeStruct(x.shape, F32),
        scratch_shapes=[
            pltpu.VMEM((tm, d), BF16), pltpu.VMEM((tm, LANES), F32),
            pltpu.VMEM((2, d, tf), F32), pltpu.VMEM((2, d, tf), F32), pltpu.VMEM((2, tf, d), F32),
            pltpu.SemaphoreType.DMA((3, 2)),
        ],
        compiler_params=_params("arbitrary", "arbitrary"),
        name="ffn",
    )(x, shift, scale, gate, nw, wi, wo)


def _proj_in_body(x_ref, sh_ref, sc_ref, nw_ref, w_ref, gain_ref, o_ref, h_scr, rs_scr, *, n_norm_tiles):
    n = pl.program_id(2)

    @pl.when(n == 0)
    def _():
        _rms_modulate_rows(x_ref, nw_ref, sh_ref, sc_ref, h_scr, rs_scr)

    acc = _dot(h_scr[...], w_ref[...].astype(BF16))

    @pl.when(n < n_norm_tiles)
    def _():
        for hh in range(acc.shape[1] // HEAD_DIM):
            sl = slice(hh * HEAD_DIM, (hh + 1) * HEAD_DIM)
            t = acc[:, sl]
            y = t * lax.rsqrt(jnp.mean(t * t, axis=-1, keepdims=True) + EPS)
            o_ref[:, sl] = (y * gain_ref[:, sl]).astype(BF16)

    @pl.when(n >= n_norm_tiles)
    def _():
        o_ref[...] = acc.astype(BF16)


def _proj_in_call(x, shift, scale, nw, w_in, gain, layer, tm=1024, tn=512):
    g_, r_, d = x.shape
    nout = w_in.shape[2]
    vec = pl.BlockSpec((None, 1, d), lambda g, i, n: (g, 0, 0))
    body = functools.partial(_proj_in_body, n_norm_tiles=2 * NA_WIDTH // tn)
    return pl.pallas_call(
        body,
        grid=(g_, r_ // tm, nout // tn),
        in_specs=[
            pl.BlockSpec((None, tm, d), lambda g, i, n: (g, i, 0)),
            vec, vec,
            pl.BlockSpec((1, d), lambda g, i, n: (0, 0)),
            pl.BlockSpec((None, d, tn), lambda g, i, n: (layer, 0, n)),
            pl.BlockSpec((1, tn), lambda g, i, n: (0, n)),
        ],
        out_specs=pl.BlockSpec((None, tm, tn), lambda g, i, n: (g, i, n)),
        out_shape=jax.ShapeDtypeStruct((g_, r_, nout), BF16),
        scratch_shapes=[pltpu.VMEM((tm, d), BF16), pltpu.VMEM((tm, LANES), F32)],
        compiler_params=_params("arbitrary", "arbitrary", "arbitrary"),
        name="proj_in",
    )(x, shift, scale, nw, w_in, gain)


def _na_plan(rows, kh):
    assert rows % NA_GROUP_ROWS == 0 and rows >= NA_UNION_ROWS and NA_UNION_ROWS % 2 == 0
    assert NA_UNION_ROWS >= NA_GROUP_ROWS + kh - 1
    groups, entries = [], {}
    for r0 in range(0, rows, NA_GROUP_ROWS):
        u0 = min(max(r0 - kh // 2, 0), rows - NA_UNION_ROWS)
        keys = []
        for r in range(r0, r0 + NA_GROUP_ROWS):
            rs = min(max(r - kh // 2, 0), rows - kh)
            assert u0 <= rs and rs + kh <= u0 + NA_UNION_ROWS
            row = []
            for j0 in range(u0, u0 + NA_UNION_ROWS, 2):
                v0, v1 = rs <= j0 < rs + kh, rs <= j0 + 1 < rs + kh
                if v0 and v1:
                    key = ("both", j0 - r + WIN_ROWS - 1)
                elif v0:
                    key = ("lo", j0 - r + WIN_ROWS - 1)
                elif v1:
                    key = ("hi", j0 + 1 - r + WIN_ROWS - 1)
                else:
                    key = None
                if key is not None:
                    entries.setdefault(key, len(entries))
                row.append(key)
            keys.append(row)
        groups.append((r0, u0, keys))
    return groups, entries


def _bias_body(rpb_ref, o_ref, *, entries):
    h = pl.program_id(0)
    shape = (GRID_W, 2 * GRID_W)
    cq = lax.broadcasted_iota(jnp.int32, shape, 0)
    lane = lax.broadcasted_iota(jnp.int32, shape, 1)
    ck = lane & (GRID_W - 1)
    hi = lane >= GRID_W
    dc = jnp.clip(ck - cq, -(WIN_COLS - 1), WIN_COLS - 1) + (WIN_COLS - 1)
    col_start = jnp.clip(cq - WIN_COLS // 2, 0, GRID_W - WIN_COLS)
    col_in = (ck >= col_start) & (ck < col_start + WIN_COLS)
    for (kind, t), slot in entries.items():
        t_lo, t_hi = (t, t + 1) if kind == "both" else (t, t)
        tile = jnp.zeros(shape, F32)
        for j in range(2 * WIN_COLS - 1):
            tile = jnp.where(dc == j, jnp.where(hi, rpb_ref[h, t_hi, j], rpb_ref[h, t_lo, j]), tile)
        valid = col_in if kind == "both" else (col_in & hi if kind == "hi" else col_in & ~hi)
        o_ref[slot] = jnp.where(valid, tile, -jnp.inf)


def _bias_call(rpb, entries):
    nh = rpb.shape[0]
    nslot = len(entries)
    return pl.pallas_call(
        functools.partial(_bias_body, entries=entries),
        grid=(nh,),
        in_specs=[pl.BlockSpec(memory_space=pltpu.SMEM)],
        out_specs=pl.BlockSpec((None, nslot, GRID_W, 2 * GRID_W), lambda h: (h, 0, 0, 0)),
        out_shape=jax.ShapeDtypeStruct((nh, nslot, GRID_W, 2 * GRID_W), F32),
        compiler_params=_params("arbitrary"),
        name="bias",
    )(rpb)


def _na_body(q_ref, k_ref, v_ref, kc_ref, vc_ref, bt_ref, o_ref, p_scr, *, groups, entries):
    kc = kc_ref[...]
    vc = vc_ref[...]
    lc = kc.shape[0]
    pair = 2 * GRID_W
    nwin = NA_UNION_ROWS * GRID_W
    for gi, (r0, u0, keys) in enumerate(groups):
        p_buf = p_scr.at[gi % 2]
        qs = slice(r0 * GRID_W, (r0 + NA_GROUP_ROWS) * GRID_W)
        us = slice(u0 * GRID_W, (u0 + NA_UNION_ROWS) * GRID_W)
        q = q_ref[qs, :]
        s_w = _dot_nt(q, k_ref[us, :])
        s_c = _dot_nt(q, kc)
        inv_l = []
        for i, row in enumerate(keys):
            rsl = slice(i * GRID_W, (i + 1) * GRID_W)
            sc_i = s_c[rsl]
            tiles = {p: s_w[rsl, p * pair:(p + 1) * pair] + bt_ref[entries[key]]
                     for p, key in enumerate(row) if key is not None}
            m_el = sc_i[:, :pair]
            for c0 in range(pair, lc, pair):
                m_el = jnp.maximum(m_el, sc_i[:, c0:c0 + pair])
            for t in tiles.values():
                m_el = jnp.maximum(m_el, t)
            m = jnp.max(m_el, axis=-1, keepdims=True)
            e_c = jnp.exp(sc_i - m)
            p_buf[rsl, nwin:] = e_c.astype(BF16)
            l_el = e_c[:, :pair]
            for c0 in range(pair, lc, pair):
                l_el = l_el + e_c[:, c0:c0 + pair]
            for p in range(len(row)):
                if p in tiles:
                    e = jnp.exp(tiles[p] - m)
                    l_el = l_el + e
                    p_buf[rsl, p * pair:(p + 1) * pair] = e.astype(BF16)
                else:
                    p_buf[rsl, p * pair:(p + 1) * pair] = jnp.zeros((GRID_W, pair), BF16)
            inv_l.append(1.0 / jnp.sum(l_el, axis=-1, keepdims=True))
        o = _dot(p_buf[:, :nwin], v_ref[us, :]) + _dot(p_buf[:, nwin:], vc)
        o_ref[qs, :] = (o * jnp.concatenate(inv_l, axis=0)).astype(BF16)


def _na_call(hl, hc, bt, layer, plan):
    b_, n_, _ = hl.shape
    lc = hc.shape[1]
    groups, entries = plan
    nslot = len(entries)
    kofs = NA_WIDTH // HEAD_DIM
    body = functools.partial(_na_body, groups=groups, entries=entries)
    return pl.pallas_call(
        body,
        grid=(b_, N_NA_HEADS),
        in_specs=[
            pl.BlockSpec((None, n_, HEAD_DIM), lambda b, h: (b, 0, h)),
            pl.BlockSpec((None, n_, HEAD_DIM), lambda b, h: (b, 0, kofs + h)),
            pl.BlockSpec((None, n_, HEAD_DIM), lambda b, h: (b, 0, 2 * kofs + h)),
            pl.BlockSpec((None, lc, HEAD_DIM), lambda b, h: (b, 0, kofs + h)),
            pl.BlockSpec((None, lc, HEAD_DIM), lambda b, h: (b, 0, 2 * kofs + h)),
            pl.BlockSpec((None, nslot, GRID_W, 2 * GRID_W), lambda b, h: (layer * N_NA_HEADS + h, 0, 0, 0)),
        ],
        out_specs=pl.BlockSpec((None, n_, HEAD_DIM), lambda b, h: (b, 0, h)),
        out_shape=jax.ShapeDtypeStruct((b_, n_, NA_WIDTH), BF16),
        scratch_shapes=[pltpu.VMEM((2, NA_GROUP_ROWS * GRID_W, NA_UNION_ROWS * GRID_W + lc), BF16)],
        compiler_params=_params("arbitrary", "arbitrary"),
        name="na",
    )(hl, hl, hl, hc, hc, bt)


def _ctx_attn_body(q_ref, k_ref, v_ref, o_ref):
    for h in range(N_NA_HEADS):
        sl = slice(h * HEAD_DIM, (h + 1) * HEAD_DIM)
        s = _dot_nt(q_ref[:, sl], k_ref[:, sl])
        m = jnp.max(s, axis=-1, keepdims=True)
        e = jnp.exp(s - m)
        l = jnp.sum(e, axis=-1, keepdims=True)
        o_ref[:, sl] = (_dot(e.astype(BF16), v_ref[:, sl]) / l).astype(BF16)


def _ctx_attn_call(hc):
    b_, lc, _ = hc.shape
    return pl.pallas_call(
        _ctx_attn_body,
        grid=(b_,),
        in_specs=[
            pl.BlockSpec((None, lc, NA_WIDTH), lambda b: (b, 0, 0)),
            pl.BlockSpec((None, lc, NA_WIDTH), lambda b: (b, 0, 1)),
            pl.BlockSpec((None, lc, NA_WIDTH), lambda b: (b, 0, 2)),
        ],
        out_specs=pl.BlockSpec((None, lc, NA_WIDTH), lambda b: (b, 0, 0)),
        out_shape=jax.ShapeDtypeStruct((b_, lc, NA_WIDTH), BF16),
        compiler_params=_params("arbitrary"),
        name="ctx_attn",
    )(hc, hc, hc)


def _dft_tables(length):
    def cos_sin(freq, n_pos, period):
        ang = ((freq[:, None] * jnp.arange(n_pos, dtype=jnp.int32)[None, :]) % period).astype(F32)
        ang = ang * (2.0 * math.pi / period)
        return jnp.cos(ang), jnp.sin(ang)

    assert length % DFT_ROW_BLOCK == 0
    coarse = jnp.arange(length // DFT_ROW_BLOCK, dtype=jnp.int32) * DFT_ROW_BLOCK
    fine = jnp.arange(DFT_ROW_BLOCK, dtype=jnp.int32)
    ca, sa = cos_sin(coarse, length, length)
    cb, sb = cos_sin(fine, length, length)
    cc, sc = cos_sin(jnp.arange(FOUR_DIM, dtype=jnp.int32), FOUR_DIM, FOUR_DIM)
    scale = 1.0 / math.sqrt(length * FOUR_DIM)
    chan = jnp.concatenate([cc * scale, sc * scale], axis=1).astype(BF16)
    return ca, sa, cb, sb, chan


def _fourier_body(f_ref, chan_ref, ca_ref, sa_ref, cb_ref, sb_ref, wf_ref, o_ref, cl_scr, sl_scr, zc_scr, zs_scr):
    @pl.when(pl.program_id(0) == 0)
    def _():
        cb = cb_ref[...]
        sb = sb_ref[...]
        for a in range(ca_ref.shape[0]):
            rows = slice(a * DFT_ROW_BLOCK, (a + 1) * DFT_ROW_BLOCK)
            ca = ca_ref[a:a + 1, :]
            sa = sa_ref[a:a + 1, :]
            cl_scr[rows, :] = (ca * cb - sa * sb).astype(BF16)
            sl_scr[rows, :] = (-(sa * cb + ca * sb)).astype(BF16)

    for g in range(N_FOUR_GROUPS):
        sl = slice(g * FOUR_DIM, (g + 1) * FOUR_DIM)
        z = _dot(f_ref[:, sl], chan_ref[...])
        zc_scr[:, sl] = z[:, :FOUR_DIM].astype(BF16)
        zs_scr[:, sl] = z[:, FOUR_DIM:].astype(BF16)
    mixed = _dot(cl_scr[...], zc_scr[...]) + _dot(sl_scr[...], zs_scr[...])
    for g in range(N_FOUR_GROUPS):
        sl = slice(g * FOUR_DIM, (g + 1) * FOUR_DIM)
        o_ref[:, sl] = _dot(mixed[:, sl].astype(BF16), wf_ref[g].astype(BF16)).astype(BF16)


def _fourier_call(h, tables, w_four, layer):
    b_, length, _ = h.shape
    ca, sa, cb, sb, chan = tables
    fblk = 3 * NA_WIDTH // FOUR_WIDTH
    whole = lambda arr: pl.BlockSpec(arr.shape, lambda b: (0,) * arr.ndim)
    return pl.pallas_call(
        _fourier_body,
        grid=(b_,),
        in_specs=[
            pl.BlockSpec((None, length, FOUR_WIDTH), lambda b: (b, 0, fblk)),
            whole(chan), whole(ca), whole(sa), whole(cb), whole(sb),
            pl.BlockSpec((None, N_FOUR_GROUPS, FOUR_DIM, FOUR_DIM), lambda b: (layer, 0, 0, 0)),
        ],
        out_specs=pl.BlockSpec((None, length, FOUR_WIDTH), lambda b: (b, 0, 0)),
        out_shape=jax.ShapeDtypeStruct((b_, length, FOUR_WIDTH), BF16),
        scratch_shapes=[pltpu.VMEM((length, length), BF16), pltpu.VMEM((length, length), BF16),
                        pltpu.VMEM((length, FOUR_WIDTH), BF16), pltpu.VMEM((length, FOUR_WIDTH), BF16)],
        compiler_params=_params("arbitrary"),
        name="fourier",
    )(h, chan, ca, sa, cb, sb, w_four)


def _proj_out_body(x_ref, g_ref, na_ref, fo_ref, w_ref, o_ref, wb_scr):
    @pl.when((pl.program_id(0) == 0) & (pl.program_id(1) == 0))
    def _():
        wb_scr[...] = w_ref[...].astype(BF16)

    y = _dot(na_ref[...], wb_scr[:NA_WIDTH, :]) + _dot(fo_ref[...], wb_scr[NA_WIDTH:, :])
    o_ref[...] = x_ref[...] + g_ref[...] * y


def _proj_out_call(x, gate, na, fo, w_out, layer, tm=512):
    g_, r_, d = x.shape
    kin = w_out.shape[1]
    return pl.pallas_call(
        _proj_out_body,
        grid=(g_, r_ // tm),
        in_specs=[
            pl.BlockSpec((None, tm, d), lambda g, i: (g, i, 0)),
            pl.BlockSpec((None, 1, d), lambda g, i: (g, 0, 0)),
            pl.BlockSpec((None, tm, NA_WIDTH), lambda g, i: (g, i, 0)),
            pl.BlockSpec((None, tm, FOUR_WIDTH), lambda g, i: (g, i, 0)),
            pl.BlockSpec((None, kin, d), lambda g, i: (layer, 0, 0), pipeline_mode=pl.Buffered(1)),
        ],
        out_specs=pl.BlockSpec((None, tm, d), lambda g, i: (g, i, 0)),
        out_shape=jax.ShapeDtypeStruct(x.shape, F32),
        scratch_shapes=[pltpu.VMEM((kin, d), BF16)],
        compiler_params=_params("arbitrary", "arbitrary"),
        name="proj_out",
    )(x, gate, na, fo, w_out)


def kernel(x, c, ctx, c_ctx, w_mod, b_mod, norm_w, ffn1_wi, ffn1_wo, w_in, q_norm_w, k_norm_w, rpb,
           w_four, w_out, ffn2_wi, ffn2_wo):
    b_, n_, d = x.shape
    lc = ctx.shape[1]
    depth = w_mod.shape[0]
    rows = n_ // GRID_W

    cvec = jnp.zeros((MOD_ROWS, d), F32).at[:b_].set(c).at[b_].set(c_ctx)
    mod = _mod_call(cvec, w_mod, b_mod).reshape(depth, MOD_ROWS, N_MOD, d)

    plan = _na_plan(rows, min(WIN_ROWS, rows))
    bt = _bias_call(rpb.reshape(depth * N_NA_HEADS, 2 * WIN_ROWS - 1, 2 * WIN_COLS - 1), plan[1])
    tables_lat = _dft_tables(n_)
    tables_ctx = _dft_tables(lc)
    ones = jnp.ones((NA_WIDTH + FOUR_WIDTH,), F32)

    xc = ctx.reshape(1, b_ * lc, d)
    for l in range(depth):
        last = l == depth - 1
        m = [mod[l, :b_, i].reshape(b_, 1, d) for i in range(N_MOD)]
        mc = [mod[l, b_, i].reshape(1, 1, d) for i in range(N_MOD)]
        nw = [norm_w[l, i].reshape(1, d) for i in range(3)]
        gain = jnp.concatenate([
            jnp.tile(q_norm_w[l] * HEAD_DIM ** -0.5, N_NA_HEADS),
            jnp.tile(k_norm_w[l], N_NA_HEADS), ones]).reshape(1, IN_WIDTH)

        x = _ffn_call(x, m[0], m[1], m[2], nw[0], ffn1_wi, ffn1_wo, l)
        xc = _ffn_call(xc, mc[0], mc[1], mc[2], nw[0], ffn1_wi, ffn1_wo, l)

        hl = _proj_in_call(x, m[3], m[4], nw[1], w_in, gain, l)
        hc = _proj_in_call(xc, mc[3], mc[4], nw[1], w_in, gain, l).reshape(b_, lc, IN_WIDTH)

        na = _na_call(hl, hc, bt, l, plan)
        fo = _fourier_call(hl, tables_lat, w_four, l)
        x = _proj_out_call(x, m[5], na, fo, w_out, l)

        if not last:
            nac = _ctx_attn_call(hc).reshape(1, b_ * lc, NA_WIDTH)
            foc = _fourier_call(hc, tables_ctx, w_four, l).reshape(1, b_ * lc, FOUR_WIDTH)
            xc = _proj_out_call(xc, mc[5], nac, foc, w_out, l)
            xc = _ffn_call(xc, mc[6], mc[7], mc[8], nw[2], ffn2_wi, ffn2_wo, l)

        x = _ffn_call(x, m[6], m[7], m[8], nw[2], ffn2_wi, ffn2_wo, l)
    return x
```

```python
import functools
import math

import jax
import jax.numpy as jnp
from jax import lax
from jax.experimental import pallas as pl
from jax.experimental.pallas import tpu as pltpu

GRID_W = 64
HEAD_DIM = 128
N_NA_HEADS = 12
NA_WIDTH = N_NA_HEADS * HEAD_DIM
N_FOUR_GROUPS = 4
FOUR_DIM = 128
FOUR_WIDTH = N_FOUR_GROUPS * FOUR_DIM
IN_WIDTH = 3 * NA_WIDTH + FOUR_WIDTH
WIN_ROWS = 8
WIN_COLS = 16
N_MOD = 9
EPS = 1e-6

BF16 = jnp.bfloat16
F32 = jnp.float32

V7X_VMEM_LIMIT_BYTES = 56 * 1024 * 1024
MOD_ROWS = 8
RMS_CHUNK_ROWS = 64
LANES = 128
DFT_ROW_BLOCK = 64
NA_GROUP_ROWS = 4
NA_SCORE_LOOKAHEAD = 2
NA_UNION_ROWS = 12


def _params(*sem):
    return pltpu.CompilerParams(dimension_semantics=sem, vmem_limit_bytes=V7X_VMEM_LIMIT_BYTES)


def _dot(a, b):
    return jnp.dot(a, b, preferred_element_type=F32)


def _dot_nt(a, b):
    return lax.dot_general(a, b, (((1,), (1,)), ((), ())), preferred_element_type=F32)


def _silu(x):
    return x / (1.0 + jnp.exp(-x))


def _rms_modulate_rows(x_ref, nw_ref, sh_ref, sc_ref, h_ref, rs_ref, copy_ref=None):
    n_chunks = x_ref.shape[0] // RMS_CHUNK_ROWS

    def rows_of(c):
        return pl.ds(pl.multiple_of(c * RMS_CHUNK_ROWS, RMS_CHUNK_ROWS), RMS_CHUNK_ROWS)

    def row_scale(c, carry):
        x = x_ref[rows_of(c), :]
        rs = lax.rsqrt(jnp.mean(x * x, axis=-1, keepdims=True) + EPS)
        rs_ref[rows_of(c), :] = jnp.broadcast_to(rs, (RMS_CHUNK_ROWS, rs_ref.shape[1]))
        return carry

    lax.fori_loop(0, n_chunks, row_scale, 0, unroll=4)

    def scale_rows(c, carry):
        rs = rs_ref[rows_of(c), :]
        for c0 in range(0, x_ref.shape[1], LANES):
            cols = slice(c0, c0 + LANES)
            x = x_ref[rows_of(c), cols]
            y = ((x * rs) * nw_ref[:, cols]) * (1.0 + sc_ref[:, cols]) + sh_ref[:, cols]
            h_ref[rows_of(c), cols] = y.astype(BF16)
            if copy_ref is not None:
                copy_ref[rows_of(c), cols] = x
        return carry

    lax.fori_loop(0, n_chunks, scale_rows, 0, unroll=2)


def _mod_body(c_ref, w_ref, b_ref, o_ref):
    s = _silu(c_ref[...]).astype(BF16)
    o_ref[...] = _dot(s, w_ref[...].astype(BF16)) + b_ref[...]


def _mod_call(cvec, w_mod, b_mod, tn=2048):
    depth, d, nout = w_mod.shape
    return pl.pallas_call(
        _mod_body,
        grid=(depth, nout // tn),
        in_specs=[
            pl.BlockSpec((MOD_ROWS, d), lambda l, j: (0, 0)),
            pl.BlockSpec((None, d, tn), lambda l, j: (l, 0, j)),
            pl.BlockSpec((None, 1, tn), lambda l, j: (l, 0, j)),
        ],
        out_specs=pl.BlockSpec((None, MOD_ROWS, tn), lambda l, j: (l, 0, j)),
        out_shape=jax.ShapeDtypeStruct((depth, MOD_ROWS, nout), F32),
        compiler_params=_params("arbitrary", "arbitrary"),
        name="mod",
    )(cvec, w_mod, b_mod.reshape(depth, 1, nout))


def _ffn_body(x_ref, sh_ref, sc_ref, g_ref, nw_ref, wi_hbm, wo_hbm, o_ref,
              h_scr, rs_scr, wg_buf, wu_buf, wo_buf, sem, *, layer, tf, nf, grid):
    tile = pl.program_id(0) * grid[1] + pl.program_id(1)
    n_tiles = grid[0] * grid[1]
    ff = nf * tf

    def weight_copies(f, slot):
        c0 = pl.multiple_of(f * tf, tf)
        c1 = pl.multiple_of(ff + f * tf, tf)
        return (
            pltpu.make_async_copy(wi_hbm.at[layer, :, pl.ds(c0, tf)], wg_buf.at[slot], sem.at[0, slot]),
            pltpu.make_async_copy(wi_hbm.at[layer, :, pl.ds(c1, tf)], wu_buf.at[slot], sem.at[1, slot]),
            pltpu.make_async_copy(wo_hbm.at[layer, pl.ds(c0, tf), :], wo_buf.at[slot], sem.at[2, slot]),
        )

    @pl.when(tile == 0)
    def _():
        for cp in weight_copies(0, 0):
            cp.start()

    _rms_modulate_rows(x_ref, nw_ref, sh_ref, sc_ref, h_scr, rs_scr, copy_ref=o_ref)
    gate = 0.5 * g_ref[...]

    def sub_step(f, slot, f_next, has_next):
        for cp in weight_copies(f, slot):
            cp.wait()

        @pl.when(has_next)
        def _():
            for cp in weight_copies(f_next, 1 - slot):
                cp.start()

        h = h_scr[...]
        g = _dot(h, wg_buf[slot].astype(BF16))
        u = _dot(h, wu_buf[slot].astype(BF16))
        a = (_silu(g) * u).astype(BF16)
        o_ref[...] += gate * _dot(a, wo_buf[slot].astype(BF16))

    def pair(p, carry):
        f0 = 2 * p
        sub_step(f0, 0, f0 + 1, True)
        sub_step(f0 + 1, 1, f0 + 2, True)
        return carry

    lax.fori_loop(0, nf // 2 - 1, pair, 0)
    sub_step(nf - 2, 0, nf - 1, True)
    sub_step(nf - 1, 1, 0, tile + 1 < n_tiles)


def _ffn_call(x, shift, scale, gate, nw, wi, wo, layer, tm=1024, tf=256):
    g_, r_, d = x.shape
    ff = wo.shape[1]
    nf = ff // tf
    assert nf % 2 == 0 and nf * tf == ff
    vec = pl.BlockSpec((None, 1, d), lambda g, i: (g, 0, 0))
    grid = (g_, r_ // tm)
    return pl.pallas_call(
        functools.partial(_ffn_body, layer=layer, tf=tf, nf=nf, grid=grid),
        grid=grid,
        in_specs=[
            pl.BlockSpec((None, tm, d), lambda g, i: (g, i, 0)),
            vec, vec, vec,
            pl.BlockSpec((1, d), lambda g, i: (0, 0)),
            pl.BlockSpec(memory_space=pl.ANY),
            pl.BlockSpec(memory_space=pl.ANY),
        ],
        out_specs=pl.BlockSpec((None, tm, d), lambda g, i: (g, i, 0)),
        out_shape=jax.ShapeDtypeStruct(x.shape, F32),
        scratch_shapes=[
            pltpu.VMEM((tm, d), BF16), pltpu.VMEM((tm, LANES), F32),
            pltpu.VMEM((2, d, tf), F32), pltpu.VMEM((2, d, tf), F32), pltpu.VMEM((2, tf, d), F32),
            pltpu.SemaphoreType.DMA((3, 2)),
        ],
        compiler_params=_params("arbitrary", "arbitrary"),
        name="ffn",
    )(x, shift, scale, gate, nw, wi, wo)


def _proj_in_body(x_ref, sh_ref, sc_ref, nw_ref, w_hbm, gain_ref, o_ref, h_scr, rs_scr, w_buf, sem,
                  *, layer, tn, n_norm_tiles, grid):
    tile = pl.program_id(0) * grid[1] + pl.program_id(1)
    n_tiles = grid[0] * grid[1]
    n_cols = o_ref.shape[1] // tn

    def w_copy(n, slot):
        return pltpu.make_async_copy(w_hbm.at[layer, :, pl.ds(n * tn, tn)], w_buf.at[slot], sem.at[slot])

    @pl.when(tile == 0)
    def _():
        w_copy(0, 0).start()

    _rms_modulate_rows(x_ref, nw_ref, sh_ref, sc_ref, h_scr, rs_scr)

    for n in range(n_cols):
        slot = n % 2
        w_copy(n, slot).wait()
        if n + 1 < n_cols:
            w_copy(n + 1, 1 - slot).start()
        else:
            @pl.when(tile + 1 < n_tiles)
            def _():
                w_copy(0, 1 - slot).start()

        acc = _dot(h_scr[...], w_buf[slot].astype(BF16))
        if n < n_norm_tiles:
            for hh in range(tn // HEAD_DIM):
                sl = slice(hh * HEAD_DIM, (hh + 1) * HEAD_DIM)
                osl = slice(n * tn + hh * HEAD_DIM, n * tn + (hh + 1) * HEAD_DIM)
                t = acc[:, sl]
                y = t * lax.rsqrt(jnp.mean(t * t, axis=-1, keepdims=True) + EPS)
                o_ref[:, osl] = (y * gain_ref[:, osl]).astype(BF16)
        else:
            o_ref[:, n * tn:(n + 1) * tn] = acc.astype(BF16)


def _proj_in_call(x, shift, scale, nw, w_in, gain, layer, tm=1024, tn=512):
    g_, r_, d = x.shape
    nout = w_in.shape[2]
    assert (nout // tn) % 2 == 0 and (2 * NA_WIDTH) % tn == 0
    grid = (g_, r_ // tm)
    vec = pl.BlockSpec((None, 1, d), lambda g, i: (g, 0, 0))
    body = functools.partial(_proj_in_body, layer=layer, tn=tn, n_norm_tiles=2 * NA_WIDTH // tn, grid=grid)
    return pl.pallas_call(
        body,
        grid=grid,
        in_specs=[
            pl.BlockSpec((None, tm, d), lambda g, i: (g, i, 0)),
            vec, vec,
            pl.BlockSpec((1, d), lambda g, i: (0, 0)),
            pl.BlockSpec(memory_space=pl.ANY),
            pl.BlockSpec((1, nout), lambda g, i: (0, 0)),
        ],
        out_specs=pl.BlockSpec((None, tm, nout), lambda g, i: (g, i, 0)),
        out_shape=jax.ShapeDtypeStruct((g_, r_, nout), BF16),
        scratch_shapes=[pltpu.VMEM((tm, d), BF16), pltpu.VMEM((tm, LANES), F32),
                        pltpu.VMEM((2, d, tn), F32), pltpu.SemaphoreType.DMA((2,))],
        compiler_params=_params("arbitrary", "arbitrary"),
        name="proj_in",
    )(x, shift, scale, nw, w_in, gain)


def _na_plan(rows, kh):
    assert rows % NA_GROUP_ROWS == 0 and rows >= NA_UNION_ROWS and NA_UNION_ROWS % 2 == 0
    assert NA_UNION_ROWS >= NA_GROUP_ROWS + kh - 1
    groups, entries = [], {}
    for r0 in range(0, rows, NA_GROUP_ROWS):
        u0 = min(max(r0 - kh // 2, 0), rows - NA_UNION_ROWS)
        keys = []
        for r in range(r0, r0 + NA_GROUP_ROWS):
            rs = min(max(r - kh // 2, 0), rows - kh)
            assert u0 <= rs and rs + kh <= u0 + NA_UNION_ROWS
            row = []
            for j0 in range(u0, u0 + NA_UNION_ROWS, 2):
                v0, v1 = rs <= j0 < rs + kh, rs <= j0 + 1 < rs + kh
                if v0 and v1:
                    key = ("both", j0 - r + WIN_ROWS - 1)
                elif v0:
                    key = ("lo", j0 - r + WIN_ROWS - 1)
                elif v1:
                    key = ("hi", j0 + 1 - r + WIN_ROWS - 1)
                else:
                    key = None
                if key is not None:
                    entries.setdefault(key, len(entries))
                row.append(key)
            keys.append(row)
        groups.append((r0, u0, keys))
    return groups, entries


def _bias_body(rpb_ref, o_ref, *, entries):
    h = pl.program_id(0)
    shape = (GRID_W, 2 * GRID_W)
    cq = lax.broadcasted_iota(jnp.int32, shape, 0)
    lane = lax.broadcasted_iota(jnp.int32, shape, 1)
    ck = lane & (GRID_W - 1)
    hi = lane >= GRID_W
    dc = jnp.clip(ck - cq, -(WIN_COLS - 1), WIN_COLS - 1) + (WIN_COLS - 1)
    col_start = jnp.clip(cq - WIN_COLS // 2, 0, GRID_W - WIN_COLS)
    col_in = (ck >= col_start) & (ck < col_start + WIN_COLS)
    for (kind, t), slot in entries.items():
        t_lo, t_hi = (t, t + 1) if kind == "both" else (t, t)
        tile = jnp.zeros(shape, F32)
        for j in range(2 * WIN_COLS - 1):
            tile = jnp.where(dc == j, jnp.where(hi, rpb_ref[h, t_hi, j], rpb_ref[h, t_lo, j]), tile)
        valid = col_in if kind == "both" else (col_in & hi if kind == "hi" else col_in & ~hi)
        o_ref[slot] = jnp.where(valid, tile, -jnp.inf)


def _bias_call(rpb, entries):
    nh = rpb.shape[0]
    nslot = len(entries)
    return pl.pallas_call(
        functools.partial(_bias_body, entries=entries),
        grid=(nh,),
        in_specs=[pl.BlockSpec(memory_space=pltpu.SMEM)],
        out_specs=pl.BlockSpec((None, nslot, GRID_W, 2 * GRID_W), lambda h: (h, 0, 0, 0)),
        out_shape=jax.ShapeDtypeStruct((nh, nslot, GRID_W, 2 * GRID_W), F32),
        compiler_params=_params("arbitrary"),
        name="bias",
    )(rpb)


def _na_body(q_ref, k_ref, v_ref, kc_ref, vc_ref, bt_ref, o_ref, *, groups, entries):
    kc = kc_ref[...]
    vc = vc_ref[...]
    lc = kc.shape[0]
    pair = 2 * GRID_W
    nwin = NA_UNION_ROWS * GRID_W

    def slices(group):
        r0, u0, _ = group
        return (slice(r0 * GRID_W, (r0 + NA_GROUP_ROWS) * GRID_W),
                slice(u0 * GRID_W, (u0 + NA_UNION_ROWS) * GRID_W))

    def scores(group):
        qs, us = slices(group)
        q = q_ref[qs, :]
        return _dot_nt(q, k_ref[us, :]), _dot_nt(q, kc)

    def finish(group, s_w, s_c):
        qs, us = slices(group)
        inv_l, p_rows = [], []
        for i, row in enumerate(group[2]):
            rsl = slice(i * GRID_W, (i + 1) * GRID_W)
            sc_i = s_c[rsl]
            tiles = {p: s_w[rsl, p * pair:(p + 1) * pair] + bt_ref[entries[key]]
                     for p, key in enumerate(row) if key is not None}
            m_el = sc_i[:, :pair]
            for c0 in range(pair, lc, pair):
                m_el = jnp.maximum(m_el, sc_i[:, c0:c0 + pair])
            for t in tiles.values():
                m_el = jnp.maximum(m_el, t)
            m = jnp.max(m_el, axis=-1, keepdims=True)
            e_c = jnp.exp(sc_i - m)
            l_el = e_c[:, :pair]
            for c0 in range(pair, lc, pair):
                l_el = l_el + e_c[:, c0:c0 + pair]
            parts = []
            for p in range(len(row)):
                if p in tiles:
                    e = jnp.exp(tiles[p] - m)
                    l_el = l_el + e
                    parts.append(e.astype(BF16))
                else:
                    parts.append(jnp.zeros((GRID_W, pair), BF16))
            parts.append(e_c.astype(BF16))
            p_rows.append(jnp.concatenate(parts, axis=1))
            inv_l.append(1.0 / jnp.sum(l_el, axis=-1, keepdims=True))
        p_all = jnp.concatenate(p_rows, axis=0)
        o = _dot(p_all[:, :nwin], v_ref[us, :]) + _dot(p_all[:, nwin:], vc)
        o_ref[qs, :] = (o * jnp.concatenate(inv_l, axis=0)).astype(BF16)

    pending = [scores(g) for g in groups[:NA_SCORE_LOOKAHEAD]]
    for gi, group in enumerate(groups):
        if gi + NA_SCORE_LOOKAHEAD < len(groups):
            pending.append(scores(groups[gi + NA_SCORE_LOOKAHEAD]))
        finish(group, *pending.pop(0))


def _na_call(hl, hc, bt, layer, plan):
    b_, n_, _ = hl.shape
    lc = hc.shape[1]
    groups, entries = plan
    nslot = len(entries)
    kofs = NA_WIDTH // HEAD_DIM
    body = functools.partial(_na_body, groups=groups, entries=entries)
    return pl.pallas_call(
        body,
        grid=(b_, N_NA_HEADS),
        in_specs=[
            pl.BlockSpec((None, n_, HEAD_DIM), lambda b, h: (b, 0, h)),
            pl.BlockSpec((None, n_, HEAD_DIM), lambda b, h: (b, 0, kofs + h)),
            pl.BlockSpec((None, n_, HEAD_DIM), lambda b, h: (b, 0, 2 * kofs + h)),
            pl.BlockSpec((None, lc, HEAD_DIM), lambda b, h: (b, 0, kofs + h)),
            pl.BlockSpec((None, lc, HEAD_DIM), lambda b, h: (b, 0, 2 * kofs + h)),
            pl.BlockSpec((None, nslot, GRID_W, 2 * GRID_W), lambda b, h: (layer * N_NA_HEADS + h, 0, 0, 0)),
        ],
        out_specs=pl.BlockSpec((None, n_, HEAD_DIM), lambda b, h: (b, 0, h)),
        out_shape=jax.ShapeDtypeStruct((b_, n_, NA_WIDTH), BF16),
        compiler_params=_params("arbitrary", "arbitrary"),
        name="na",
    )(hl, hl, hl, hc, hc, bt)


def _ctx_attn_body(q_ref, k_ref, v_ref, o_ref):
    for h in range(N_NA_HEADS):
        sl = slice(h * HEAD_DIM, (h + 1) * HEAD_DIM)
        s = _dot_nt(q_ref[:, sl], k_ref[:, sl])
        m = jnp.max(s, axis=-1, keepdims=True)
        e = jnp.exp(s - m)
        l = jnp.sum(e, axis=-1, keepdims=True)
        o_ref[:, sl] = (_dot(e.astype(BF16), v_ref[:, sl]) / l).astype(BF16)


def _ctx_attn_call(hc):
    b_, lc, _ = hc.shape
    return pl.pallas_call(
        _ctx_attn_body,
        grid=(b_,),
        in_specs=[
            pl.BlockSpec((None, lc, NA_WIDTH), lambda b: (b, 0, 0)),
            pl.BlockSpec((None, lc, NA_WIDTH), lambda b: (b, 0, 1)),
            pl.BlockSpec((None, lc, NA_WIDTH), lambda b: (b, 0, 2)),
        ],
        out_specs=pl.BlockSpec((None, lc, NA_WIDTH), lambda b: (b, 0, 0)),
        out_shape=jax.ShapeDtypeStruct((b_, lc, NA_WIDTH), BF16),
        compiler_params=_params("arbitrary"),
        name="ctx_attn",
    )(hc, hc, hc)


def _dft_tables(length):
    def cos_sin(freq, n_pos, period):
        ang = ((freq[:, None] * jnp.arange(n_pos, dtype=jnp.int32)[None, :]) % period).astype(F32)
        ang = ang * (2.0 * math.pi / period)
        return jnp.cos(ang), jnp.sin(ang)

    assert length % DFT_ROW_BLOCK == 0
    coarse = jnp.arange(length // DFT_ROW_BLOCK, dtype=jnp.int32) * DFT_ROW_BLOCK
    fine = jnp.arange(DFT_ROW_BLOCK, dtype=jnp.int32)
    ca, sa = cos_sin(coarse, length, length)
    cb, sb = cos_sin(fine, length, length)
    cc, sc = cos_sin(jnp.arange(FOUR_DIM, dtype=jnp.int32), FOUR_DIM, FOUR_DIM)
    scale = 1.0 / math.sqrt(length * FOUR_DIM)
    chan = jnp.concatenate([cc * scale, sc * scale], axis=1).astype(BF16)
    return ca, sa, cb, sb, chan


def _fourier_body(f_ref, chan_ref, ca_ref, sa_ref, cb_ref, sb_ref, wf_ref, o_ref, cl_scr, sl_scr, zc_scr, zs_scr):
    @pl.when(pl.program_id(0) == 0)
    def _():
        cb = cb_ref[...]
        sb = sb_ref[...]
        for a in range(ca_ref.shape[0]):
            rows = slice(a * DFT_ROW_BLOCK, (a + 1) * DFT_ROW_BLOCK)
            ca = ca_ref[a:a + 1, :]
            sa = sa_ref[a:a + 1, :]
            cl_scr[rows, :] = (ca * cb - sa * sb).astype(BF16)
            sl_scr[rows, :] = (-(sa * cb + ca * sb)).astype(BF16)

    for g in range(N_FOUR_GROUPS):
        sl = slice(g * FOUR_DIM, (g + 1) * FOUR_DIM)
        z = _dot(f_ref[:, sl], chan_ref[...])
        zc_scr[:, sl] = z[:, :FOUR_DIM].astype(BF16)
        zs_scr[:, sl] = z[:, FOUR_DIM:].astype(BF16)
    mixed = _dot(cl_scr[...], zc_scr[...]) + _dot(sl_scr[...], zs_scr[...])
    for g in range(N_FOUR_GROUPS):
        sl = slice(g * FOUR_DIM, (g + 1) * FOUR_DIM)
        o_ref[:, sl] = _dot(mixed[:, sl].astype(BF16), wf_ref[g].astype(BF16)).astype(BF16)


def _fourier_call(h, tables, w_four, layer):
    b_, length, _ = h.shape
    ca, sa, cb, sb, chan = tables
    fblk = 3 * NA_WIDTH // FOUR_WIDTH
    whole = lambda arr: pl.BlockSpec(arr.shape, lambda b: (0,) * arr.ndim)
    return pl.pallas_call(
        _fourier_body,
        grid=(b_,),
        in_specs=[
            pl.BlockSpec((None, length, FOUR_WIDTH), lambda b: (b, 0, fblk)),
            whole(chan), whole(ca), whole(sa), whole(cb), whole(sb),
            pl.BlockSpec((None, N_FOUR_GROUPS, FOUR_DIM, FOUR_DIM), lambda b: (layer, 0, 0, 0)),
        ],
        out_specs=pl.BlockSpec((None, length, FOUR_WIDTH), lambda b: (b, 0, 0)),
        out_shape=jax.ShapeDtypeStruct((b_, length, FOUR_WIDTH), BF16),
        scratch_shapes=[pltpu.VMEM((length, length), BF16), pltpu.VMEM((length, length), BF16),
                        pltpu.VMEM((length, FOUR_WIDTH), BF16), pltpu.VMEM((length, FOUR_WIDTH), BF16)],
        compiler_params=_params("arbitrary"),
        name="fourier",
    )(h, chan, ca, sa, cb, sb, w_four)


def _proj_out_body(x_ref, g_ref, na_ref, fo_ref, w_ref, o_ref, wb_scr):
    @pl.when((pl.program_id(0) == 0) & (pl.program_id(1) == 0))
    def _():
        wb_scr[...] = w_ref[...].astype(BF16)

    y = _dot(na_ref[...], wb_scr[:NA_WIDTH, :]) + _dot(fo_ref[...], wb_scr[NA_WIDTH:, :])
    o_ref[...] = x_ref[...] + g_ref[...] * y


def _proj_out_call(x, gate, na, fo, w_out, layer, tm=512):
    g_, r_, d = x.shape
    kin = w_out.shape[1]
    return pl.pallas_call(
        _proj_out_body,
        grid=(g_, r_ // tm),
        in_specs=[
            pl.BlockSpec((None, tm, d), lambda g, i: (g, i, 0)),
            pl.BlockSpec((None, 1, d), lambda g, i: (g, 0, 0)),
            pl.BlockSpec((None, tm, NA_WIDTH), lambda g, i: (g, i, 0)),
            pl.BlockSpec((None, tm, FOUR_WIDTH), lambda g, i: (g, i, 0)),
            pl.BlockSpec((None, kin, d), lambda g, i: (layer, 0, 0), pipeline_mode=pl.Buffered(1)),
        ],
        out_specs=pl.BlockSpec((None, tm, d), lambda g, i: (g, i, 0)),
        out_shape=jax.ShapeDtypeStruct(x.shape, F32),
        scratch_shapes=[pltpu.VMEM((kin, d), BF16)],
        compiler_params=_params("arbitrary", "arbitrary"),
        name="proj_out",
    )(x, gate, na, fo, w_out)


def kernel(x, c, ctx, c_ctx, w_mod, b_mod, norm_w, ffn1_wi, ffn1_wo, w_in, q_norm_w, k_norm_w, rpb,
           w_four, w_out, ffn2_wi, ffn2_wo):
    b_, n_, d = x.shape
    lc = ctx.shape[1]
    depth = w_mod.shape[0]
    rows = n_ // GRID_W

    cvec = jnp.zeros((MOD_ROWS, d), F32).at[:b_].set(c).at[b_].set(c_ctx)
    mod = _mod_call(cvec, w_mod, b_mod).reshape(depth, MOD_ROWS, N_MOD, d)

    plan = _na_plan(rows, min(WIN_ROWS, rows))
    bt = _bias_call(rpb.reshape(depth * N_NA_HEADS, 2 * WIN_ROWS - 1, 2 * WIN_COLS - 1), plan[1])
    tables_lat = _dft_tables(n_)
    tables_ctx = _dft_tables(lc)
    ones = jnp.ones((NA_WIDTH + FOUR_WIDTH,), F32)

    xc = ctx.reshape(1, b_ * lc, d)
    for l in range(depth):
        last = l == depth - 1
        m = [mod[l, :b_, i].reshape(b_, 1, d) for i in range(N_MOD)]
        mc = [mod[l, b_, i].reshape(1, 1, d) for i in range(N_MOD)]
        nw = [norm_w[l, i].reshape(1, d) for i in range(3)]
        gain = jnp.concatenate([
            jnp.tile(q_norm_w[l] * HEAD_DIM ** -0.5, N_NA_HEADS),
            jnp.tile(k_norm_w[l], N_NA_HEADS), ones]).reshape(1, IN_WIDTH)

        x = _ffn_call(x, m[0], m[1], m[2], nw[0], ffn1_wi, ffn1_wo, l)
        xc = _ffn_call(xc, mc[0], mc[1], mc[2], nw[0], ffn1_wi, ffn1_wo, l)

        hl = _proj_in_call(x, m[3], m[4], nw[1], w_in, gain, l)
        hc = _proj_in_call(xc, mc[3], mc[4], nw[1], w_in, gain, l).reshape(b_, lc, IN_WIDTH)

        na = _na_call(hl, hc, bt, l, plan)
        fo = _fourier_call(hl, tables_lat, w_four, l)
        x = _proj_out_call(x, m[5], na, fo, w_out, l)

        if not last:
            nac = _ctx_attn_call(hc).reshape(1, b_ * lc, NA_WIDTH)
            foc = _fourier_call(hc, tables_ctx, w_four, l).reshape(1, b_ * lc, FOUR_WIDTH)
            xc = _proj_out_call(xc, mc[5], nac, foc, w_out, l)
            xc = _ffn_call(xc, mc[6], mc[7], mc[8], nw[2], ffn2_wi, ffn2_wo, l)

        x = _ffn_call(x, m[6], m[7], m[8], nw[2], ffn2_wi, ffn2_wo, l)
    return x
```

```python
import functools
import math

import jax
import jax.numpy as jnp
from jax import lax
from jax.experimental import pallas as pl
from jax.experimental.pallas import tpu as pltpu

GRID_W = 64
HEAD_DIM = 128
N_NA_HEADS = 12
NA_WIDTH = N_NA_HEADS * HEAD_DIM
N_FOUR_GROUPS = 4
FOUR_DIM = 128
FOUR_WIDTH = N_FOUR_GROUPS * FOUR_DIM
IN_WIDTH = 3 * NA_WIDTH + FOUR_WIDTH
WIN_ROWS = 8
WIN_COLS = 16
N_MOD = 9
EPS = 1e-6

BF16 = jnp.bfloat16
F32 = jnp.float32

V7X_VMEM_LIMIT_BYTES = 56 * 1024 * 1024
MOD_ROWS = 8
RMS_CHUNK_ROWS = 64
LANES = 128
DFT_ROW_BLOCK = 64
NA_GROUP_ROWS = 4
NA_SCORE_LOOKAHEAD = 2
NA_UNION_ROWS = 12


def _params(*sem):
    return pltpu.CompilerParams(dimension_semantics=sem, vmem_limit_bytes=V7X_VMEM_LIMIT_BYTES)


def _dot(a, b):
    return jnp.dot(a, b, preferred_element_type=F32)


def _dot_nt(a, b):
    return lax.dot_general(a, b, (((1,), (1,)), ((), ())), preferred_element_type=F32)


def _silu(x):
    return x / (1.0 + jnp.exp(-x))


def _rms_modulate_rows(x_ref, nw_ref, sh_ref, sc_ref, h_ref, rs_ref):
    n_chunks = x_ref.shape[0] // RMS_CHUNK_ROWS

    def rows_of(c):
        return pl.ds(pl.multiple_of(c * RMS_CHUNK_ROWS, RMS_CHUNK_ROWS), RMS_CHUNK_ROWS)

    def row_scale(c, carry):
        x = x_ref[rows_of(c), :]
        rs = lax.rsqrt(jnp.mean(x * x, axis=-1, keepdims=True) + EPS)
        rs_ref[rows_of(c), :] = jnp.broadcast_to(rs, (RMS_CHUNK_ROWS, rs_ref.shape[1]))
        return carry

    lax.fori_loop(0, n_chunks, row_scale, 0, unroll=4)

    def scale_rows(c, carry):
        rs = rs_ref[rows_of(c), :]
        for c0 in range(0, x_ref.shape[1], LANES):
            cols = slice(c0, c0 + LANES)
            x = x_ref[rows_of(c), cols]
            y = ((x * rs) * nw_ref[:, cols]) * (1.0 + sc_ref[:, cols]) + sh_ref[:, cols]
            h_ref[rows_of(c), cols] = y.astype(BF16)
        return carry

    lax.fori_loop(0, n_chunks, scale_rows, 0, unroll=2)


def _mod_body(c_ref, w_ref, b_ref, o_ref):
    s = _silu(c_ref[...]).astype(BF16)
    o_ref[...] = _dot(s, w_ref[...].astype(BF16)) + b_ref[...]


def _mod_call(cvec, w_mod, b_mod, tn=2048):
    depth, d, nout = w_mod.shape
    return pl.pallas_call(
        _mod_body,
        grid=(depth, nout // tn),
        in_specs=[
            pl.BlockSpec((MOD_ROWS, d), lambda l, j: (0, 0)),
            pl.BlockSpec((None, d, tn), lambda l, j: (l, 0, j)),
            pl.BlockSpec((None, 1, tn), lambda l, j: (l, 0, j)),
        ],
        out_specs=pl.BlockSpec((None, MOD_ROWS, tn), lambda l, j: (l, 0, j)),
        out_shape=jax.ShapeDtypeStruct((depth, MOD_ROWS, nout), F32),
        compiler_params=_params("arbitrary", "arbitrary"),
        name="mod",
    )(cvec, w_mod, b_mod.reshape(depth, 1, nout))


def _ffn_body(x_ref, sh_ref, sc_ref, g_ref, nw_ref, wi_hbm, wo_hbm, o_ref,
              h_scr, rs_scr, wg_buf, wu_buf, wo_buf, sem, *, layer, tf, nf, grid):
    tile = pl.program_id(0) * grid[1] + pl.program_id(1)
    n_tiles = grid[0] * grid[1]
    ff = nf * tf
    half = tf // 2

    def weight_copies(f, slot):
        c0 = f * tf
        return (
            (pltpu.make_async_copy(wi_hbm.at[layer, :, pl.ds(c0, tf)], wg_buf.at[slot], sem.at[0, slot]), 0),
            (pltpu.make_async_copy(wi_hbm.at[layer, :, pl.ds(ff + c0, tf)], wu_buf.at[slot], sem.at[1, slot]), 1),
            (pltpu.make_async_copy(wo_hbm.at[layer, pl.ds(c0, half), :],
                                   wo_buf.at[slot, pl.ds(0, half), :], sem.at[2, slot]), 0),
            (pltpu.make_async_copy(wo_hbm.at[layer, pl.ds(c0 + half, half), :],
                                   wo_buf.at[slot, pl.ds(half, half), :], sem.at[3, slot]), 1),
        )

    def start_weights(f, slot):
        for cp, priority in weight_copies(f, slot):
            cp.start(priority=priority)

    @pl.when(tile == 0)
    def _():
        start_weights(0, 0)

    _rms_modulate_rows(x_ref, nw_ref, sh_ref, sc_ref, h_scr, rs_scr)
    gate = 0.5 * g_ref[...]

    for f in range(nf):
        slot = f % 2
        for cp, _ in weight_copies(f, slot):
            cp.wait()
        if f + 1 < nf:
            start_weights(f + 1, 1 - slot)
        else:
            @pl.when(tile + 1 < n_tiles)
            def _():
                start_weights(0, 1 - slot)

        h = h_scr[...]
        g = _dot(h, wg_buf[slot].astype(BF16))
        u = _dot(h, wu_buf[slot].astype(BF16))
        a = (_silu(g) * u).astype(BF16)
        y = gate * _dot(a, wo_buf[slot].astype(BF16))
        if f == 0:
            o_ref[...] = x_ref[...] + y
        else:
            o_ref[...] += y


def _ffn_call(x, shift, scale, gate, nw, wi, wo, layer, tm=1024, tf=256):
    g_, r_, d = x.shape
    ff = wo.shape[1]
    nf = ff // tf
    assert nf % 2 == 0 and nf * tf == ff
    vec = pl.BlockSpec((None, 1, d), lambda g, i: (g, 0, 0))
    grid = (g_, r_ // tm)
    return pl.pallas_call(
        functools.partial(_ffn_body, layer=layer, tf=tf, nf=nf, grid=grid),
        grid=grid,
        in_specs=[
            pl.BlockSpec((None, tm, d), lambda g, i: (g, i, 0)),
            vec, vec, vec,
            pl.BlockSpec((1, d), lambda g, i: (0, 0)),
            pl.BlockSpec(memory_space=pl.ANY),
            pl.BlockSpec(memory_space=pl.ANY),
        ],
        out_specs=pl.BlockSpec((None, tm, d), lambda g, i: (g, i, 0)),
        out_shape=jax.ShapeDtypeStruct(x.shape, F32),
        scratch_shapes=[
            pltpu.VMEM((tm, d), BF16), pltpu.VMEM((tm, LANES), F32),
            pltpu.VMEM((2, d, tf), F32), pltpu.VMEM((2, d, tf), F32), pltpu.VMEM((2, tf, d), F32),
            pltpu.SemaphoreType.DMA((4, 2)),
        ],
        compiler_params=_params("arbitrary", "arbitrary"),
        name="ffn",
    )(x, shift, scale, gate, nw, wi, wo)


def _proj_in_body(x_ref, sh_ref, sc_ref, nw_ref, w_hbm, gain_ref, o_ref, h_scr, rs_scr, w_buf, sem,
                  *, layer, tn, n_norm_tiles, grid):
    tile = pl.program_id(0) * grid[1] + pl.program_id(1)
    n_tiles = grid[0] * grid[1]
    n_cols = o_ref.shape[1] // tn

    half = w_buf.shape[1] // 2

    def w_copies(n, slot):
        return tuple(
            pltpu.make_async_copy(w_hbm.at[layer, pl.ds(part * half, half), pl.ds(n * tn, tn)],
                                  w_buf.at[slot, pl.ds(part * half, half), :], sem.at[part, slot])
            for part in range(2))

    def start_weights(n, slot):
        for priority, cp in enumerate(w_copies(n, slot)):
            cp.start(priority=priority)

    @pl.when(tile == 0)
    def _():
        start_weights(0, 0)

    _rms_modulate_rows(x_ref, nw_ref, sh_ref, sc_ref, h_scr, rs_scr)

    for n in range(n_cols):
        slot = n % 2
        for cp in w_copies(n, slot):
            cp.wait()
        if n + 1 < n_cols:
            start_weights(n + 1, 1 - slot)
        else:
            @pl.when(tile + 1 < n_tiles)
            def _():
                start_weights(0, 1 - slot)

        acc = _dot(h_scr[...], w_buf[slot].astype(BF16))
        if n < n_norm_tiles:
            for hh in range(tn // HEAD_DIM):
                sl = slice(hh * HEAD_DIM, (hh + 1) * HEAD_DIM)
                osl = slice(n * tn + hh * HEAD_DIM, n * tn + (hh + 1) * HEAD_DIM)
                t = acc[:, sl]
                y = t * lax.rsqrt(jnp.mean(t * t, axis=-1, keepdims=True) + EPS)
                o_ref[:, osl] = (y * gain_ref[:, osl]).astype(BF16)
        else:
            o_ref[:, n * tn:(n + 1) * tn] = acc.astype(BF16)


def _proj_in_call(x, shift, scale, nw, w_in, gain, layer, tm=1024, tn=512):
    g_, r_, d = x.shape
    nout = w_in.shape[2]
    assert (nout // tn) % 2 == 0 and (2 * NA_WIDTH) % tn == 0
    grid = (g_, r_ // tm)
    vec = pl.BlockSpec((None, 1, d), lambda g, i: (g, 0, 0))
    body = functools.partial(_proj_in_body, layer=layer, tn=tn, n_norm_tiles=2 * NA_WIDTH // tn, grid=grid)
    return pl.pallas_call(
        body,
        grid=grid,
        in_specs=[
            pl.BlockSpec((None, tm, d), lambda g, i: (g, i, 0)),
            vec, vec,
            pl.BlockSpec((1, d), lambda g, i: (0, 0)),
            pl.BlockSpec(memory_space=pl.ANY),
            pl.BlockSpec((1, nout), lambda g, i: (0, 0)),
        ],
        out_specs=pl.BlockSpec((None, tm, nout), lambda g, i: (g, i, 0)),
        out_shape=jax.ShapeDtypeStruct((g_, r_, nout), BF16),
        scratch_shapes=[pltpu.VMEM((tm, d), BF16), pltpu.VMEM((tm, LANES), F32),
                        pltpu.VMEM((2, d, tn), F32), pltpu.SemaphoreType.DMA((2, 2))],
        compiler_params=_params("arbitrary", "arbitrary"),
        name="proj_in",
    )(x, shift, scale, nw, w_in, gain)


def _na_plan(rows, kh):
    assert rows % NA_GROUP_ROWS == 0 and rows >= NA_UNION_ROWS and NA_UNION_ROWS % 2 == 0
    assert NA_UNION_ROWS >= NA_GROUP_ROWS + kh - 1
    groups, entries = [], {}
    for r0 in range(0, rows, NA_GROUP_ROWS):
        u0 = min(max(r0 - kh // 2, 0), rows - NA_UNION_ROWS)
        keys = []
        for r in range(r0, r0 + NA_GROUP_ROWS):
            rs = min(max(r - kh // 2, 0), rows - kh)
            assert u0 <= rs and rs + kh <= u0 + NA_UNION_ROWS
            row = []
            for j0 in range(u0, u0 + NA_UNION_ROWS, 2):
                v0, v1 = rs <= j0 < rs + kh, rs <= j0 + 1 < rs + kh
                if v0 and v1:
                    key = ("both", j0 - r + WIN_ROWS - 1)
                elif v0:
                    key = ("lo", j0 - r + WIN_ROWS - 1)
                elif v1:
                    key = ("hi", j0 + 1 - r + WIN_ROWS - 1)
                else:
                    key = None
                if key is not None:
                    entries.setdefault(key, len(entries))
                row.append(key)
            keys.append(row)
        groups.append((r0, u0, keys))
    return groups, entries


def _bias_body(rpb_ref, o_ref, *, entries):
    h = pl.program_id(0)
    shape = (GRID_W, 2 * GRID_W)
    cq = lax.broadcasted_iota(jnp.int32, shape, 0)
    lane = lax.broadcasted_iota(jnp.int32, shape, 1)
    ck = lane & (GRID_W - 1)
    hi = lane >= GRID_W
    dc = jnp.clip(ck - cq, -(WIN_COLS - 1), WIN_COLS - 1) + (WIN_COLS - 1)
    col_start = jnp.clip(cq - WIN_COLS // 2, 0, GRID_W - WIN_COLS)
    col_in = (ck >= col_start) & (ck < col_start + WIN_COLS)
    for (kind, t), slot in entries.items():
        t_lo, t_hi = (t, t + 1) if kind == "both" else (t, t)
        tile = jnp.zeros(shape, F32)
        for j in range(2 * WIN_COLS - 1):
            tile = jnp.where(dc == j, jnp.where(hi, rpb_ref[h, t_hi, j], rpb_ref[h, t_lo, j]), tile)
        valid = col_in if kind == "both" else (col_in & hi if kind == "hi" else col_in & ~hi)
        o_ref[slot] = jnp.where(valid, tile, -jnp.inf)


def _bias_call(rpb, entries):
    nh = rpb.shape[0]
    nslot = len(entries)
    return pl.pallas_call(
        functools.partial(_bias_body, entries=entries),
        grid=(nh,),
        in_specs=[pl.BlockSpec(memory_space=pltpu.SMEM)],
        out_specs=pl.BlockSpec((None, nslot, GRID_W, 2 * GRID_W), lambda h: (h, 0, 0, 0)),
        out_shape=jax.ShapeDtypeStruct((nh, nslot, GRID_W, 2 * GRID_W), F32),
        compiler_params=_params("arbitrary"),
        name="bias",
    )(rpb)


def _na_body(q_ref, k_ref, v_ref, kc_ref, vc_ref, bt_ref, o_ref, *, groups, entries):
    kc = kc_ref[...]
    vc = vc_ref[...]
    lc = kc.shape[0]
    pair = 2 * GRID_W
    nwin = NA_UNION_ROWS * GRID_W

    def slices(group):
        r0, u0, _ = group
        return (slice(r0 * GRID_W, (r0 + NA_GROUP_ROWS) * GRID_W),
                slice(u0 * GRID_W, (u0 + NA_UNION_ROWS) * GRID_W))

    def scores(group):
        qs, us = slices(group)
        q = q_ref[qs, :]
        return _dot_nt(q, k_ref[us, :]), _dot_nt(q, kc)

    def finish(group, s_w, s_c):
        qs, us = slices(group)
        inv_l, p_rows = [], []
        for i, row in enumerate(group[2]):
            rsl = slice(i * GRID_W, (i + 1) * GRID_W)
            sc_i = s_c[rsl]
            tiles = {p: s_w[rsl, p * pair:(p + 1) * pair] + bt_ref[entries[key]]
                     for p, key in enumerate(row) if key is not None}
            m_el = sc_i[:, :pair]
            for c0 in range(pair, lc, pair):
                m_el = jnp.maximum(m_el, sc_i[:, c0:c0 + pair])
            for t in tiles.values():
                m_el = jnp.maximum(m_el, t)
            m = jnp.max(m_el, axis=-1, keepdims=True)
            e_c = jnp.exp(sc_i - m)
            l_el = e_c[:, :pair]
            for c0 in range(pair, lc, pair):
                l_el = l_el + e_c[:, c0:c0 + pair]
            parts = []
            for p in range(len(row)):
                if p in tiles:
                    e = jnp.exp(tiles[p] - m)
                    l_el = l_el + e
                    parts.append(e.astype(BF16))
                else:
                    parts.append(jnp.zeros((GRID_W, pair), BF16))
            parts.append(e_c.astype(BF16))
            p_rows.append(jnp.concatenate(parts, axis=1))
            inv_l.append(1.0 / jnp.sum(l_el, axis=-1, keepdims=True))
        p_all = jnp.concatenate(p_rows, axis=0)
        o = _dot(p_all[:, :nwin], v_ref[us, :]) + _dot(p_all[:, nwin:], vc)
        o_ref[qs, :] = (o * jnp.concatenate(inv_l, axis=0)).astype(BF16)

    pending = [scores(g) for g in groups[:NA_SCORE_LOOKAHEAD]]
    for gi, group in enumerate(groups):
        if gi + NA_SCORE_LOOKAHEAD < len(groups):
            pending.append(scores(groups[gi + NA_SCORE_LOOKAHEAD]))
        finish(group, *pending.pop(0))


def _na_call(hl, hc, bt, layer, plan):
    b_, n_, _ = hl.shape
    lc = hc.shape[1]
    groups, entries = plan
    nslot = len(entries)
    kofs = NA_WIDTH // HEAD_DIM
    body = functools.partial(_na_body, groups=groups, entries=entries)
    return pl.pallas_call(
        body,
        grid=(b_, N_NA_HEADS),
        in_specs=[
            pl.BlockSpec((None, n_, HEAD_DIM), lambda b, h: (b, 0, h)),
            pl.BlockSpec((None, n_, HEAD_DIM), lambda b, h: (b, 0, kofs + h)),
            pl.BlockSpec((None, n_, HEAD_DIM), lambda b, h: (b, 0, 2 * kofs + h)),
            pl.BlockSpec((None, lc, HEAD_DIM), lambda b, h: (b, 0, kofs + h)),
            pl.BlockSpec((None, lc, HEAD_DIM), lambda b, h: (b, 0, 2 * kofs + h)),
            pl.BlockSpec((None, nslot, GRID_W, 2 * GRID_W), lambda b, h: (layer * N_NA_HEADS + h, 0, 0, 0)),
        ],
        out_specs=pl.BlockSpec((None, n_, HEAD_DIM), lambda b, h: (b, 0, h)),
        out_shape=jax.ShapeDtypeStruct((b_, n_, NA_WIDTH), BF16),
        compiler_params=_params("arbitrary", "arbitrary"),
        name="na",
    )(hl, hl, hl, hc, hc, bt)


def _ctx_attn_body(q_ref, k_ref, v_ref, o_ref):
    for h in range(N_NA_HEADS):
        sl = slice(h * HEAD_DIM, (h + 1) * HEAD_DIM)
        s = _dot_nt(q_ref[:, sl], k_ref[:, sl])
        m = jnp.max(s, axis=-1, keepdims=True)
        e = jnp.exp(s - m)
        l = jnp.sum(e, axis=-1, keepdims=True)
        o_ref[:, sl] = (_dot(e.astype(BF16), v_ref[:, sl]) / l).astype(BF16)


def _ctx_attn_call(hc):
    b_, lc, _ = hc.shape
    return pl.pallas_call(
        _ctx_attn_body,
        grid=(b_,),
        in_specs=[
            pl.BlockSpec((None, lc, NA_WIDTH), lambda b: (b, 0, 0)),
            pl.BlockSpec((None, lc, NA_WIDTH), lambda b: (b, 0, 1)),
            pl.BlockSpec((None, lc, NA_WIDTH), lambda b: (b, 0, 2)),
        ],
        out_specs=pl.BlockSpec((None, lc, NA_WIDTH), lambda b: (b, 0, 0)),
        out_shape=jax.ShapeDtypeStruct((b_, lc, NA_WIDTH), BF16),
        compiler_params=_params("arbitrary"),
        name="ctx_attn",
    )(hc, hc, hc)


def _dft_tables(length):
    def cos_sin(freq, n_pos, period):
        ang = ((freq[:, None] * jnp.arange(n_pos, dtype=jnp.int32)[None, :]) % period).astype(F32)
        ang = ang * (2.0 * math.pi / period)
        return jnp.cos(ang), jnp.sin(ang)

    assert length % DFT_ROW_BLOCK == 0
    coarse = jnp.arange(length // DFT_ROW_BLOCK, dtype=jnp.int32) * DFT_ROW_BLOCK
    fine = jnp.arange(DFT_ROW_BLOCK, dtype=jnp.int32)
    ca, sa = cos_sin(coarse, length, length)
    cb, sb = cos_sin(fine, length, length)
    cc, sc = cos_sin(jnp.arange(FOUR_DIM, dtype=jnp.int32), FOUR_DIM, FOUR_DIM)
    scale = 1.0 / math.sqrt(length * FOUR_DIM)
    chan = jnp.concatenate([cc * scale, sc * scale], axis=1).astype(BF16)
    return ca, sa, cb, sb, chan


def _fourier_body(f_ref, chan_ref, ca_ref, sa_ref, cb_ref, sb_ref, wf_ref, o_ref, cl_scr, sl_scr, zc_scr, zs_scr):
    @pl.when(pl.program_id(0) == 0)
    def _():
        cb = cb_ref[...]
        sb = sb_ref[...]
        for a in range(ca_ref.shape[0]):
            rows = slice(a * DFT_ROW_BLOCK, (a + 1) * DFT_ROW_BLOCK)
            ca = ca_ref[a:a + 1, :]
            sa = sa_ref[a:a + 1, :]
            cl_scr[rows, :] = (ca * cb - sa * sb).astype(BF16)
            sl_scr[rows, :] = (-(sa * cb + ca * sb)).astype(BF16)

    for g in range(N_FOUR_GROUPS):
        sl = slice(g * FOUR_DIM, (g + 1) * FOUR_DIM)
        z = _dot(f_ref[:, sl], chan_ref[...])
        zc_scr[:, sl] = z[:, :FOUR_DIM].astype(BF16)
        zs_scr[:, sl] = z[:, FOUR_DIM:].astype(BF16)
    mixed = _dot(cl_scr[...], zc_scr[...]) + _dot(sl_scr[...], zs_scr[...])
    for g in range(N_FOUR_GROUPS):
        sl = slice(g * FOUR_DIM, (g + 1) * FOUR_DIM)
        o_ref[:, sl] = _dot(mixed[:, sl].astype(BF16), wf_ref[g].astype(BF16)).astype(BF16)


def _fourier_call(h, tables, w_four, layer):
    b_, length, _ = h.shape
    ca, sa, cb, sb, chan = tables
    fblk = 3 * NA_WIDTH // FOUR_WIDTH
    whole = lambda arr: pl.BlockSpec(arr.shape, lambda b: (0,) * arr.ndim)
    return pl.pallas_call(
        _fourier_body,
        grid=(b_,),
        in_specs=[
            pl.BlockSpec((None, length, FOUR_WIDTH), lambda b: (b, 0, fblk)),
            whole(chan), whole(ca), whole(sa), whole(cb), whole(sb),
            pl.BlockSpec((None, N_FOUR_GROUPS, FOUR_DIM, FOUR_DIM), lambda b: (layer, 0, 0, 0)),
        ],
        out_specs=pl.BlockSpec((None, length, FOUR_WIDTH), lambda b: (b, 0, 0)),
        out_shape=jax.ShapeDtypeStruct((b_, length, FOUR_WIDTH), BF16),
        scratch_shapes=[pltpu.VMEM((length, length), BF16), pltpu.VMEM((length, length), BF16),
                        pltpu.VMEM((length, FOUR_WIDTH), BF16), pltpu.VMEM((length, FOUR_WIDTH), BF16)],
        compiler_params=_params("arbitrary"),
        name="fourier",
    )(h, chan, ca, sa, cb, sb, w_four)


def _proj_out_body(x_ref, g_ref, na_ref, fo_ref, w_ref, o_ref, wb_scr):
    @pl.when((pl.program_id(0) == 0) & (pl.program_id(1) == 0))
    def _():
        wb_scr[...] = w_ref[...].astype(BF16)

    y = _dot(na_ref[...], wb_scr[:NA_WIDTH, :]) + _dot(fo_ref[...], wb_scr[NA_WIDTH:, :])
    o_ref[...] = x_ref[...] + g_ref[...] * y


def _proj_out_call(x, gate, na, fo, w_out, layer, tm=512):
    g_, r_, d = x.shape
    kin = w_out.shape[1]
    return pl.pallas_call(
        _proj_out_body,
        grid=(g_, r_ // tm),
        in_specs=[
            pl.BlockSpec((None, tm, d), lambda g, i: (g, i, 0)),
            pl.BlockSpec((None, 1, d), lambda g, i: (g, 0, 0)),
            pl.BlockSpec((None, tm, NA_WIDTH), lambda g, i: (g, i, 0)),
            pl.BlockSpec((None, tm, FOUR_WIDTH), lambda g, i: (g, i, 0)),
            pl.BlockSpec((None, kin, d), lambda g, i: (layer, 0, 0), pipeline_mode=pl.Buffered(1)),
        ],
        out_specs=pl.BlockSpec((None, tm, d), lambda g, i: (g, i, 0)),
        out_shape=jax.ShapeDtypeStruct(x.shape, F32),
        scratch_shapes=[pltpu.VMEM((kin, d), BF16)],
        compiler_params=_params("arbitrary", "arbitrary"),
        name="proj_out",
    )(x, gate, na, fo, w_out)


def kernel(x, c, ctx, c_ctx, w_mod, b_mod, norm_w, ffn1_wi, ffn1_wo, w_in, q_norm_w, k_norm_w, rpb,
           w_four, w_out, ffn2_wi, ffn2_wo):
    b_, n_, d = x.shape
    lc = ctx.shape[1]
    depth = w_mod.shape[0]
    rows = n_ // GRID_W

    cvec = jnp.zeros((MOD_ROWS, d), F32).at[:b_].set(c).at[b_].set(c_ctx)
    mod = _mod_call(cvec, w_mod, b_mod).reshape(depth, MOD_ROWS, N_MOD, d)

    plan = _na_plan(rows, min(WIN_ROWS, rows))
    bt = _bias_call(rpb.reshape(depth * N_NA_HEADS, 2 * WIN_ROWS - 1, 2 * WIN_COLS - 1), plan[1])
    tables_lat = _dft_tables(n_)
    tables_ctx = _dft_tables(lc)
    ones = jnp.ones((NA_WIDTH + FOUR_WIDTH,), F32)

    xc = ctx.reshape(1, b_ * lc, d)
    for l in range(depth):
        last = l == depth - 1
        m = [mod[l, :b_, i].reshape(b_, 1, d) for i in range(N_MOD)]
        mc = [mod[l, b_, i].reshape(1, 1, d) for i in range(N_MOD)]
        nw = [norm_w[l, i].reshape(1, d) for i in range(3)]
        gain = jnp.concatenate([
            jnp.tile(q_norm_w[l] * HEAD_DIM ** -0.5, N_NA_HEADS),
            jnp.tile(k_norm_w[l], N_NA_HEADS), ones]).reshape(1, IN_WIDTH)

        x = _ffn_call(x, m[0], m[1], m[2], nw[0], ffn1_wi, ffn1_wo, l)
        xc = _ffn_call(xc, mc[0], mc[1], mc[2], nw[0], ffn1_wi, ffn1_wo, l)

        hl = _proj_in_call(x, m[3], m[4], nw[1], w_in, gain, l)
        hc = _proj_in_call(xc, mc[3], mc[4], nw[1], w_in, gain, l).reshape(b_, lc, IN_WIDTH)

        na = _na_call(hl, hc, bt, l, plan)
        fo = _fourier_call(hl, tables_lat, w_four, l)
        x = _proj_out_call(x, m[5], na, fo, w_out, l)

        if not last:
            nac = _ctx_attn_call(hc).reshape(1, b_ * lc, NA_WIDTH)
            foc = _fourier_call(hc, tables_ctx, w_four, l).reshape(1, b_ * lc, FOUR_WIDTH)
            xc = _proj_out_call(xc, mc[5], nac, foc, w_out, l)
            xc = _ffn_call(xc, mc[6], mc[7], mc[8], nw[2], ffn2_wi, ffn2_wo, l)

        x = _ffn_call(x, m[6], m[7], m[8], nw[2], ffn2_wi, ffn2_wo, l)
    return x
```

```python
import functools
import math

import jax
import jax.numpy as jnp
from jax import lax
from jax.experimental import pallas as pl
from jax.experimental.pallas import tpu as pltpu

GRID_W = 64
HEAD_DIM = 128
N_NA_HEADS = 12
NA_WIDTH = N_NA_HEADS * HEAD_DIM
N_FOUR_GROUPS = 4
FOUR_DIM = 128
FOUR_WIDTH = N_FOUR_GROUPS * FOUR_DIM
IN_WIDTH = 3 * NA_WIDTH + FOUR_WIDTH
WIN_ROWS = 8
WIN_COLS = 16
N_MOD = 9
EPS = 1e-6

BF16 = jnp.bfloat16
F32 = jnp.float32

V7X_VMEM_LIMIT_BYTES = 56 * 1024 * 1024
MOD_ROWS = 8
RMS_CHUNK_ROWS = 64
LANES = 128
DFT_ROW_BLOCK = 64
NA_GROUP_ROWS = 4
NA_SCORE_LOOKAHEAD = 2
NA_UNION_ROWS = 12


def _params(*sem):
    return pltpu.CompilerParams(dimension_semantics=sem, vmem_limit_bytes=V7X_VMEM_LIMIT_BYTES)


def _dot(a, b):
    return jnp.dot(a, b, preferred_element_type=F32)


def _dot_nt(a, b):
    return lax.dot_general(a, b, (((1,), (1,)), ((), ())), preferred_element_type=F32)


def _silu(x):
    return x / (1.0 + jnp.exp(-x))


def _rms_modulate_rows(x_ref, nw_ref, sh_ref, sc_ref, h_ref, rs_ref, copy_ref=None):
    n_chunks = x_ref.shape[0] // RMS_CHUNK_ROWS

    def rows_of(c):
        return pl.ds(pl.multiple_of(c * RMS_CHUNK_ROWS, RMS_CHUNK_ROWS), RMS_CHUNK_ROWS)

    def row_scale(c, carry):
        x = x_ref[rows_of(c), :]
        rs = lax.rsqrt(jnp.mean(x * x, axis=-1, keepdims=True) + EPS)
        rs_ref[rows_of(c), :] = jnp.broadcast_to(rs, (RMS_CHUNK_ROWS, rs_ref.shape[1]))
        return carry

    lax.fori_loop(0, n_chunks, row_scale, 0, unroll=4)

    def scale_rows(c, carry):
        rs = rs_ref[rows_of(c), :]
        for c0 in range(0, x_ref.shape[1], LANES):
            cols = slice(c0, c0 + LANES)
            x = x_ref[rows_of(c), cols]
            y = ((x * rs) * nw_ref[:, cols]) * (1.0 + sc_ref[:, cols]) + sh_ref[:, cols]
            h_ref[rows_of(c), cols] = y.astype(BF16)
            if copy_ref is not None:
                copy_ref[rows_of(c), cols] = x
        return carry

    lax.fori_loop(0, n_chunks, scale_rows, 0, unroll=2)


def _mod_body(c_ref, w_ref, b_ref, o_ref):
    s = _silu(c_ref[...]).astype(BF16)
    o_ref[...] = _dot(s, w_ref[...].astype(BF16)) + b_ref[...]


def _mod_call(cvec, w_mod, b_mod, tn=2048):
    depth, d, nout = w_mod.shape
    return pl.pallas_call(
        _mod_body,
        grid=(depth, nout // tn),
        in_specs=[
            pl.BlockSpec((MOD_ROWS, d), lambda l, j: (0, 0)),
            pl.BlockSpec((None, d, tn), lambda l, j: (l, 0, j)),
            pl.BlockSpec((None, 1, tn), lambda l, j: (l, 0, j)),
        ],
        out_specs=pl.BlockSpec((None, MOD_ROWS, tn), lambda l, j: (l, 0, j)),
        out_shape=jax.ShapeDtypeStruct((depth, MOD_ROWS, nout), F32),
        compiler_params=_params("arbitrary", "arbitrary"),
        name="mod",
    )(cvec, w_mod, b_mod.reshape(depth, 1, nout))


def _ffn_body(x_ref, sh_ref, sc_ref, g_ref, nw_ref, wi_hbm, wo_hbm, o_ref,
              h_scr, rs_scr, wg_buf, wu_buf, wo_buf, sem, *, layer, tf, nf, grid):
    tile = pl.program_id(0) * grid[1] + pl.program_id(1)
    n_tiles = grid[0] * grid[1]
    ff = nf * tf

    def weight_copies(f, slot):
        c0 = pl.multiple_of(f * tf, tf)
        c1 = pl.multiple_of(ff + f * tf, tf)
        return (
            pltpu.make_async_copy(wi_hbm.at[layer, :, pl.ds(c0, tf)], wg_buf.at[slot], sem.at[0, slot]),
            pltpu.make_async_copy(wi_hbm.at[layer, :, pl.ds(c1, tf)], wu_buf.at[slot], sem.at[1, slot]),
            pltpu.make_async_copy(wo_hbm.at[layer, pl.ds(c0, tf), :], wo_buf.at[slot], sem.at[2, slot]),
        )

    @pl.when(tile == 0)
    def _():
        for cp in weight_copies(0, 0):
            cp.start()

    _rms_modulate_rows(x_ref, nw_ref, sh_ref, sc_ref, h_scr, rs_scr, copy_ref=o_ref)
    gate = 0.5 * g_ref[...]

    def sub_step(f, slot, f_next, has_next):
        for cp in weight_copies(f, slot):
            cp.wait()

        @pl.when(has_next)
        def _():
            for cp in weight_copies(f_next, 1 - slot):
                cp.start()

        h = h_scr[...]
        g = _dot(h, wg_buf[slot].astype(BF16))
        u = _dot(h, wu_buf[slot].astype(BF16))
        a = (_silu(g) * u).astype(BF16)
        o_ref[...] += gate * _dot(a, wo_buf[slot].astype(BF16))

    def pair(p, carry):
        f0 = 2 * p
        sub_step(f0, 0, f0 + 1, True)
        sub_step(f0 + 1, 1, f0 + 2, True)
        return carry

    lax.fori_loop(0, nf // 2 - 1, pair, 0)
    sub_step(nf - 2, 0, nf - 1, True)
    sub_step(nf - 1, 1, 0, tile + 1 < n_tiles)


def _ffn_call(x, shift, scale, gate, nw, wi, wo, layer, tm=1024, tf=256):
    g_, r_, d = x.shape
    ff = wo.shape[1]
    nf = ff // tf
    assert nf % 2 == 0 and nf * tf == ff
    vec = pl.BlockSpec((None, 1, d), lambda g, i: (g, 0, 0))
    grid = (g_, r_ // tm)
    return pl.pallas_call(
        functools.partial(_ffn_body, layer=layer, tf=tf, nf=nf, grid=grid),
        grid=grid,
        in_specs=[
            pl.BlockSpec((None, tm, d), lambda g, i: (g, i, 0)),
            vec, vec, vec,
            pl.BlockSpec((1, d), lambda g, i: (0, 0)),
            pl.BlockSpec(memory_space=pl.ANY),
            pl.BlockSpec(memory_space=pl.ANY),
        ],
        out_specs=pl.BlockSpec((None, tm, d), lambda g, i: (g, i, 0)),
        out_shape=jax.ShapeDtypeStruct(x.shape, F32),
        scratch_shapes=[
            pltpu.VMEM((tm, d), BF16), pltpu.VMEM((tm, LANES), F32),
            pltpu.VMEM((2, d, tf), F32), pltpu.VMEM((2, d, tf), F32), pltpu.VMEM((2, tf, d), F32),
            pltpu.SemaphoreType.DMA((3, 2)),
        ],
        compiler_params=_params("arbitrary", "arbitrary"),
        name="ffn",
    )(x, shift, scale, gate, nw, wi, wo)


def _proj_in_body(x_ref, sh_ref, sc_ref, nw_ref, w_hbm, gain_ref, o_ref, h_scr, rs_scr, w_buf, sem,
                  *, layer, tn, n_norm_tiles, col_tiles, grid):
    tile = pl.program_id(0) * grid[1] + pl.program_id(1)
    n_tiles = grid[0] * grid[1]
    n_lo, n_hi = col_tiles

    def w_copy(n, slot):
        return pltpu.make_async_copy(w_hbm.at[layer, :, pl.ds(n * tn, tn)], w_buf.at[slot], sem.at[slot])

    @pl.when(tile == 0)
    def _():
        w_copy(n_lo, 0).start()

    _rms_modulate_rows(x_ref, nw_ref, sh_ref, sc_ref, h_scr, rs_scr)

    for n in range(n_lo, n_hi):
        slot = (n - n_lo) % 2
        w_copy(n, slot).wait()
        if n + 1 < n_hi:
            w_copy(n + 1, 1 - slot).start()
        else:
            @pl.when(tile + 1 < n_tiles)
            def _():
                w_copy(n_lo, 1 - slot).start()

        acc = _dot(h_scr[...], w_buf[slot].astype(BF16))
        if n < n_norm_tiles:
            for hh in range(tn // HEAD_DIM):
                sl = slice(hh * HEAD_DIM, (hh + 1) * HEAD_DIM)
                gsl = slice(n * tn + hh * HEAD_DIM, n * tn + (hh + 1) * HEAD_DIM)
                osl = slice((n - n_lo) * tn + hh * HEAD_DIM, (n - n_lo) * tn + (hh + 1) * HEAD_DIM)
                t = acc[:, sl]
                y = t * lax.rsqrt(jnp.mean(t * t, axis=-1, keepdims=True) + EPS)
                o_ref[:, osl] = (y * gain_ref[:, gsl]).astype(BF16)
        else:
            o_ref[:, (n - n_lo) * tn:(n - n_lo + 1) * tn] = acc.astype(BF16)


def _proj_in_call(x, shift, scale, nw, w_in, gain, layer, col_range=None, tm=1024, tn=512):
    g_, r_, d = x.shape
    win = w_in.shape[2]
    lo, hi = (0, win) if col_range is None else col_range
    assert lo % tn == 0 and hi % tn == 0 and ((hi - lo) // tn) % 2 == 0 and (2 * NA_WIDTH) % tn == 0
    nout = hi - lo
    grid = (g_, r_ // tm)
    vec = pl.BlockSpec((None, 1, d), lambda g, i: (g, 0, 0))
    body = functools.partial(_proj_in_body, layer=layer, tn=tn, n_norm_tiles=2 * NA_WIDTH // tn,
                             col_tiles=(lo // tn, hi // tn), grid=grid)
    return pl.pallas_call(
        body,
        grid=grid,
        in_specs=[
            pl.BlockSpec((None, tm, d), lambda g, i: (g, i, 0)),
            vec, vec,
            pl.BlockSpec((1, d), lambda g, i: (0, 0)),
            pl.BlockSpec(memory_space=pl.ANY),
            pl.BlockSpec((1, win), lambda g, i: (0, 0)),
        ],
        out_specs=pl.BlockSpec((None, tm, nout), lambda g, i: (g, i, 0)),
        out_shape=jax.ShapeDtypeStruct((g_, r_, nout), BF16),
        scratch_shapes=[pltpu.VMEM((tm, d), BF16), pltpu.VMEM((tm, LANES), F32),
                        pltpu.VMEM((2, d, tn), F32), pltpu.SemaphoreType.DMA((2,))],
        compiler_params=_params("arbitrary", "arbitrary"),
        name="proj_in",
    )(x, shift, scale, nw, w_in, gain)


def _na_plan(rows, kh):
    assert rows % NA_GROUP_ROWS == 0 and rows >= NA_UNION_ROWS and NA_UNION_ROWS % 2 == 0
    assert NA_UNION_ROWS >= NA_GROUP_ROWS + kh - 1
    groups, entries = [], {}
    for r0 in range(0, rows, NA_GROUP_ROWS):
        u0 = min(max(r0 - kh // 2, 0), rows - NA_UNION_ROWS)
        keys = []
        for r in range(r0, r0 + NA_GROUP_ROWS):
            rs = min(max(r - kh // 2, 0), rows - kh)
            assert u0 <= rs and rs + kh <= u0 + NA_UNION_ROWS
            row = []
            for j0 in range(u0, u0 + NA_UNION_ROWS, 2):
                v0, v1 = rs <= j0 < rs + kh, rs <= j0 + 1 < rs + kh
                if v0 and v1:
                    key = ("both", j0 - r + WIN_ROWS - 1)
                elif v0:
                    key = ("lo", j0 - r + WIN_ROWS - 1)
                elif v1:
                    key = ("hi", j0 + 1 - r + WIN_ROWS - 1)
                else:
                    key = None
                if key is not None:
                    entries.setdefault(key, len(entries))
                row.append(key)
            keys.append(row)
        groups.append((r0, u0, keys))
    return groups, entries


def _bias_body(pairs_ref, o_ref, *, entries):
    shape = (GRID_W, 2 * GRID_W)
    n_dc = 2 * WIN_COLS - 1
    cq = lax.broadcasted_iota(jnp.int32, shape, 0)
    lane = lax.broadcasted_iota(jnp.int32, shape, 1)
    ck = lane & (GRID_W - 1)
    hi = lane >= GRID_W
    dc = jnp.clip(ck - cq, -(WIN_COLS - 1), WIN_COLS - 1) + (WIN_COLS - 1)
    idx = dc + jnp.where(hi, n_dc, 0)
    col_start = jnp.clip(cq - WIN_COLS // 2, 0, GRID_W - WIN_COLS)
    col_in = (ck >= col_start) & (ck < col_start + WIN_COLS)
    valid = {"both": col_in, "hi": col_in & hi, "lo": col_in & ~hi}
    for h in range(o_ref.shape[0]):
        for (kind, t), slot in entries.items():
            row = t - 1 if kind == "hi" else t
            src = jnp.broadcast_to(pairs_ref[h, row:row + 1, :], shape)
            tile = jnp.take_along_axis(src, idx, axis=1)
            o_ref[h, slot] = jnp.where(valid[kind], tile, -jnp.inf)


def _bias_call(rpb, entries):
    depth, heads, n_dr, n_dc = rpb.shape
    assert 2 * n_dc <= 2 * GRID_W
    nxt = jnp.concatenate([rpb[:, :, 1:], jnp.zeros_like(rpb[:, :, :1])], axis=2)
    pairs = jnp.concatenate([rpb, nxt, jnp.zeros((depth, heads, n_dr, 2 * GRID_W - 2 * n_dc), F32)], axis=3)
    nslot = len(entries)
    return pl.pallas_call(
        functools.partial(_bias_body, entries=entries),
        grid=(depth,),
        in_specs=[pl.BlockSpec((None, heads, n_dr, 2 * GRID_W), lambda l: (l, 0, 0, 0))],
        out_specs=pl.BlockSpec((heads, nslot, GRID_W, 2 * GRID_W), lambda l: (l, 0, 0, 0)),
        out_shape=jax.ShapeDtypeStruct((depth * heads, nslot, GRID_W, 2 * GRID_W), F32),
        compiler_params=_params("arbitrary"),
        name="bias",
    )(pairs)


def _na_body(q_ref, k_ref, v_ref, kc_ref, vc_ref, bt_ref, o_ref, *, groups, entries):
    kc = kc_ref[...]
    vc = vc_ref[...]
    lc = kc.shape[0]
    pair = 2 * GRID_W
    nwin = NA_UNION_ROWS * GRID_W

    def slices(group):
        r0, u0, _ = group
        return (slice(r0 * GRID_W, (r0 + NA_GROUP_ROWS) * GRID_W),
                slice(u0 * GRID_W, (u0 + NA_UNION_ROWS) * GRID_W))

    def scores(group):
        qs, us = slices(group)
        q = q_ref[qs, :]
        return _dot_nt(q, k_ref[us, :]), _dot_nt(q, kc)

    def finish(group, s_w, s_c):
        qs, us = slices(group)
        inv_l, p_rows = [], []
        for i, row in enumerate(group[2]):
            rsl = slice(i * GRID_W, (i + 1) * GRID_W)
            sc_i = s_c[rsl]
            tiles = {p: s_w[rsl, p * pair:(p + 1) * pair] + bt_ref[entries[key]]
                     for p, key in enumerate(row) if key is not None}
            m_el = sc_i[:, :pair]
            for c0 in range(pair, lc, pair):
                m_el = jnp.maximum(m_el, sc_i[:, c0:c0 + pair])
            for t in tiles.values():
                m_el = jnp.maximum(m_el, t)
            m = jnp.max(m_el, axis=-1, keepdims=True)
            e_c = jnp.exp(sc_i - m)
            l_el = e_c[:, :pair]
            for c0 in range(pair, lc, pair):
                l_el = l_el + e_c[:, c0:c0 + pair]
            parts = []
            for p in range(len(row)):
                if p in tiles:
                    e = jnp.exp(tiles[p] - m)
                    l_el = l_el + e
                    parts.append(e.astype(BF16))
                else:
                    parts.append(jnp.zeros((GRID_W, pair), BF16))
            parts.append(e_c.astype(BF16))
            p_rows.append(jnp.concatenate(parts, axis=1))
            inv_l.append(1.0 / jnp.sum(l_el, axis=-1, keepdims=True))
        p_all = jnp.concatenate(p_rows, axis=0)
        o = _dot(p_all[:, :nwin], v_ref[us, :]) + _dot(p_all[:, nwin:], vc)
        o_ref[qs, :] = (o * jnp.concatenate(inv_l, axis=0)).astype(BF16)

    pending = [scores(g) for g in groups[:NA_SCORE_LOOKAHEAD]]
    for gi, group in enumerate(groups):
        if gi + NA_SCORE_LOOKAHEAD < len(groups):
            pending.append(scores(groups[gi + NA_SCORE_LOOKAHEAD]))
        finish(group, *pending.pop(0))


def _na_call(hl, hc, ctx_k_block, bt, layer, plan):
    b_, n_, _ = hl.shape
    lc = hc.shape[1]
    groups, entries = plan
    nslot = len(entries)
    kofs = NA_WIDTH // HEAD_DIM
    body = functools.partial(_na_body, groups=groups, entries=entries)
    return pl.pallas_call(
        body,
        grid=(b_, N_NA_HEADS),
        in_specs=[
            pl.BlockSpec((None, n_, HEAD_DIM), lambda b, h: (b, 0, h)),
            pl.BlockSpec((None, n_, HEAD_DIM), lambda b, h: (b, 0, kofs + h)),
            pl.BlockSpec((None, n_, HEAD_DIM), lambda b, h: (b, 0, 2 * kofs + h)),
            pl.BlockSpec((None, lc, HEAD_DIM), lambda b, h: (b, 0, ctx_k_block + h)),
            pl.BlockSpec((None, lc, HEAD_DIM), lambda b, h: (b, 0, ctx_k_block + kofs + h)),
            pl.BlockSpec((None, nslot, GRID_W, 2 * GRID_W), lambda b, h: (layer * N_NA_HEADS + h, 0, 0, 0)),
        ],
        out_specs=pl.BlockSpec((None, n_, HEAD_DIM), lambda b, h: (b, 0, h)),
        out_shape=jax.ShapeDtypeStruct((b_, n_, NA_WIDTH), BF16),
        compiler_params=_params("arbitrary", "arbitrary"),
        name="na",
    )(hl, hl, hl, hc, hc, bt)


def _ctx_attn_body(q_ref, k_ref, v_ref, o_ref):
    for h in range(N_NA_HEADS):
        sl = slice(h * HEAD_DIM, (h + 1) * HEAD_DIM)
        s = _dot_nt(q_ref[:, sl], k_ref[:, sl])
        m = jnp.max(s, axis=-1, keepdims=True)
        e = jnp.exp(s - m)
        l = jnp.sum(e, axis=-1, keepdims=True)
        o_ref[:, sl] = (_dot(e.astype(BF16), v_ref[:, sl]) / l).astype(BF16)


def _ctx_attn_call(hc):
    b_, lc, _ = hc.shape
    return pl.pallas_call(
        _ctx_attn_body,
        grid=(b_,),
        in_specs=[
            pl.BlockSpec((None, lc, NA_WIDTH), lambda b: (b, 0, 0)),
            pl.BlockSpec((None, lc, NA_WIDTH), lambda b: (b, 0, 1)),
            pl.BlockSpec((None, lc, NA_WIDTH), lambda b: (b, 0, 2)),
        ],
        out_specs=pl.BlockSpec((None, lc, NA_WIDTH), lambda b: (b, 0, 0)),
        out_shape=jax.ShapeDtypeStruct((b_, lc, NA_WIDTH), BF16),
        compiler_params=_params("arbitrary"),
        name="ctx_attn",
    )(hc, hc, hc)


def _dft_tables(length):
    def cos_sin(freq, n_pos, period):
        ang = ((freq[:, None] * jnp.arange(n_pos, dtype=jnp.int32)[None, :]) % period).astype(F32)
        ang = ang * (2.0 * math.pi / period)
        return jnp.cos(ang), jnp.sin(ang)

    assert length % DFT_ROW_BLOCK == 0
    coarse = jnp.arange(length // DFT_ROW_BLOCK, dtype=jnp.int32) * DFT_ROW_BLOCK
    fine = jnp.arange(DFT_ROW_BLOCK, dtype=jnp.int32)
    ca, sa = cos_sin(coarse, length, length)
    cb, sb = cos_sin(fine, length, length)
    cc, sc = cos_sin(jnp.arange(FOUR_DIM, dtype=jnp.int32), FOUR_DIM, FOUR_DIM)
    scale = 1.0 / math.sqrt(length * FOUR_DIM)
    chan = jnp.concatenate([cc * scale, sc * scale], axis=1).astype(BF16)
    return ca, sa, cb, sb, chan


def _fourier_body(f_ref, chan_ref, ca_ref, sa_ref, cb_ref, sb_ref, wf_ref, o_ref, cl_scr, sl_scr, zc_scr, zs_scr):
    @pl.when(pl.program_id(0) == 0)
    def _():
        cb = cb_ref[...]
        sb = sb_ref[...]
        for a in range(ca_ref.shape[0]):
            rows = slice(a * DFT_ROW_BLOCK, (a + 1) * DFT_ROW_BLOCK)
            ca = ca_ref[a:a + 1, :]
            sa = sa_ref[a:a + 1, :]
            cl_scr[rows, :] = (ca * cb - sa * sb).astype(BF16)
            sl_scr[rows, :] = (-(sa * cb + ca * sb)).astype(BF16)

    for g in range(N_FOUR_GROUPS):
        sl = slice(g * FOUR_DIM, (g + 1) * FOUR_DIM)
        z = _dot(f_ref[:, sl], chan_ref[...])
        zc_scr[:, sl] = z[:, :FOUR_DIM].astype(BF16)
        zs_scr[:, sl] = z[:, FOUR_DIM:].astype(BF16)
    mixed = _dot(cl_scr[...], zc_scr[...]) + _dot(sl_scr[...], zs_scr[...])
    for g in range(N_FOUR_GROUPS):
        sl = slice(g * FOUR_DIM, (g + 1) * FOUR_DIM)
        o_ref[:, sl] = _dot(mixed[:, sl].astype(BF16), wf_ref[g].astype(BF16)).astype(BF16)


def _fourier_call(h, tables, w_four, layer):
    b_, length, _ = h.shape
    ca, sa, cb, sb, chan = tables
    fblk = 3 * NA_WIDTH // FOUR_WIDTH
    whole = lambda arr: pl.BlockSpec(arr.shape, lambda b: (0,) * arr.ndim)
    return pl.pallas_call(
        _fourier_body,
        grid=(b_,),
        in_specs=[
            pl.BlockSpec((None, length, FOUR_WIDTH), lambda b: (b, 0, fblk)),
            whole(chan), whole(ca), whole(sa), whole(cb), whole(sb),
            pl.BlockSpec((None, N_FOUR_GROUPS, FOUR_DIM, FOUR_DIM), lambda b: (layer, 0, 0, 0)),
        ],
        out_specs=pl.BlockSpec((None, length, FOUR_WIDTH), lambda b: (b, 0, 0)),
        out_shape=jax.ShapeDtypeStruct((b_, length, FOUR_WIDTH), BF16),
        scratch_shapes=[pltpu.VMEM((length, length), BF16), pltpu.VMEM((length, length), BF16),
                        pltpu.VMEM((length, FOUR_WIDTH), BF16), pltpu.VMEM((length, FOUR_WIDTH), BF16)],
        compiler_params=_params("arbitrary"),
        name="fourier",
    )(h, chan, ca, sa, cb, sb, w_four)


def _proj_out_body(x_ref, g_ref, na_ref, fo_ref, w_ref, o_ref, wb_scr):
    @pl.when((pl.program_id(0) == 0) & (pl.program_id(1) == 0))
    def _():
        wb_scr[...] = w_ref[...].astype(BF16)

    y = _dot(na_ref[...], wb_scr[:NA_WIDTH, :]) + _dot(fo_ref[...], wb_scr[NA_WIDTH:, :])
    o_ref[...] = x_ref[...] + g_ref[...] * y


def _proj_out_call(x, gate, na, fo, w_out, layer, tm=512):
    g_, r_, d = x.shape
    kin = w_out.shape[1]
    return pl.pallas_call(
        _proj_out_body,
        grid=(g_, r_ // tm),
        in_specs=[
            pl.BlockSpec((None, tm, d), lambda g, i: (g, i, 0)),
            pl.BlockSpec((None, 1, d), lambda g, i: (g, 0, 0)),
            pl.BlockSpec((None, tm, NA_WIDTH), lambda g, i: (g, i, 0)),
            pl.BlockSpec((None, tm, FOUR_WIDTH), lambda g, i: (g, i, 0)),
            pl.BlockSpec((None, kin, d), lambda g, i: (layer, 0, 0), pipeline_mode=pl.Buffered(1)),
        ],
        out_specs=pl.BlockSpec((None, tm, d), lambda g, i: (g, i, 0)),
        out_shape=jax.ShapeDtypeStruct(x.shape, F32),
        scratch_shapes=[pltpu.VMEM((kin, d), BF16)],
        compiler_params=_params("arbitrary", "arbitrary"),
        name="proj_out",
    )(x, gate, na, fo, w_out)


def kernel(x, c, ctx, c_ctx, w_mod, b_mod, norm_w, ffn1_wi, ffn1_wo, w_in, q_norm_w, k_norm_w, rpb,
           w_four, w_out, ffn2_wi, ffn2_wo):
    b_, n_, d = x.shape
    lc = ctx.shape[1]
    depth = w_mod.shape[0]
    rows = n_ // GRID_W

    cvec = jnp.zeros((MOD_ROWS, d), F32).at[:b_].set(c).at[b_].set(c_ctx)
    mod = _mod_call(cvec, w_mod, b_mod).reshape(depth, MOD_ROWS, N_MOD, d)

    plan = _na_plan(rows, min(WIN_ROWS, rows))
    bt = _bias_call(rpb, plan[1])
    tables_lat = _dft_tables(n_)
    tables_ctx = _dft_tables(lc)
    ones = jnp.ones((NA_WIDTH + FOUR_WIDTH,), F32)

    xc = ctx.reshape(1, b_ * lc, d)
    for l in range(depth):
        last = l == depth - 1
        m = [mod[l, :b_, i].reshape(b_, 1, d) for i in range(N_MOD)]
        mc = [mod[l, b_, i].reshape(1, 1, d) for i in range(N_MOD)]
        nw = [norm_w[l, i].reshape(1, d) for i in range(3)]
        gain = jnp.concatenate([
            jnp.tile(q_norm_w[l] * HEAD_DIM ** -0.5, N_NA_HEADS),
            jnp.tile(k_norm_w[l], N_NA_HEADS), ones]).reshape(1, IN_WIDTH)

        x = _ffn_call(x, m[0], m[1], m[2], nw[0], ffn1_wi, ffn1_wo, l)
        xc = _ffn_call(xc, mc[0], mc[1], mc[2], nw[0], ffn1_wi, ffn1_wo, l)

        hl = _proj_in_call(x, m[3], m[4], nw[1], w_in, gain, l)
        ctx_cols = (NA_WIDTH, 3 * NA_WIDTH) if last else (0, IN_WIDTH)
        hc = _proj_in_call(xc, mc[3], mc[4], nw[1], w_in, gain, l, col_range=ctx_cols)
        hc = hc.reshape(b_, lc, ctx_cols[1] - ctx_cols[0])

        na = _na_call(hl, hc, (NA_WIDTH - ctx_cols[0]) // HEAD_DIM, bt, l, plan)
        fo = _fourier_call(hl, tables_lat, w_four, l)
        x = _proj_out_call(x, m[5], na, fo, w_out, l)

        if not last:
            nac = _ctx_attn_call(hc).reshape(1, b_ * lc, NA_WIDTH)
            foc = _fourier_call(hc, tables_ctx, w_four, l).reshape(1, b_ * lc, FOUR_WIDTH)
            xc = _proj_out_call(xc, mc[5], nac, foc, w_out, l)
            xc = _ffn_call(xc, mc[6], mc[7], mc[8], nw[2], ffn2_wi, ffn2_wo, l)

        x = _ffn_call(x, m[6], m[7], m[8], nw[2], ffn2_wi, ffn2_wo, l)
    return x
```

```python
import functools
import math

import jax
import jax.numpy as jnp
from jax import lax
from jax.experimental import pallas as pl
from jax.experimental.pallas import tpu as pltpu

GRID_W = 64
HEAD_DIM = 128
N_NA_HEADS = 12
NA_WIDTH = N_NA_HEADS * HEAD_DIM
N_FOUR_GROUPS = 4
FOUR_DIM = 128
FOUR_WIDTH = N_FOUR_GROUPS * FOUR_DIM
IN_WIDTH = 3 * NA_WIDTH + FOUR_WIDTH
WIN_ROWS = 8
WIN_COLS = 16
N_MOD = 9
EPS = 1e-6

BF16 = jnp.bfloat16
F32 = jnp.float32

V7X_VMEM_LIMIT_BYTES = 56 * 1024 * 1024
MOD_ROWS = 8
RMS_CHUNK_ROWS = 64
LANES = 128
DFT_ROW_BLOCK = 64
NA_GROUP_ROWS = 4
NA_SCORE_LOOKAHEAD = 2
NA_UNION_ROWS = 12


def _params(*sem):
    return pltpu.CompilerParams(dimension_semantics=sem, vmem_limit_bytes=V7X_VMEM_LIMIT_BYTES)


def _dot(a, b):
    return jnp.dot(a, b, preferred_element_type=F32)


def _dot_nt(a, b):
    return lax.dot_general(a, b, (((1,), (1,)), ((), ())), preferred_element_type=F32)


def _silu(x):
    return x / (1.0 + jnp.exp(-x))


def _rms_modulate_rows(x_ref, nw_ref, sh_ref, sc_ref, h_ref, rs_ref, copy_ref=None):
    n_chunks = x_ref.shape[0] // RMS_CHUNK_ROWS

    def rows_of(c):
        return pl.ds(pl.multiple_of(c * RMS_CHUNK_ROWS, RMS_CHUNK_ROWS), RMS_CHUNK_ROWS)

    def row_scale(c, carry):
        x = x_ref[rows_of(c), :]
        rs = lax.rsqrt(jnp.mean(x * x, axis=-1, keepdims=True) + EPS)
        rs_ref[rows_of(c), :] = jnp.broadcast_to(rs, (RMS_CHUNK_ROWS, rs_ref.shape[1]))
        return carry

    lax.fori_loop(0, n_chunks, row_scale, 0, unroll=4)

    def scale_rows(c, carry):
        rs = rs_ref[rows_of(c), :]
        for c0 in range(0, x_ref.shape[1], LANES):
            cols = slice(c0, c0 + LANES)
            x = x_ref[rows_of(c), cols]
            y = ((x * rs) * nw_ref[:, cols]) * (1.0 + sc_ref[:, cols]) + sh_ref[:, cols]
            h_ref[rows_of(c), cols] = y.astype(BF16)
            if copy_ref is not None:
                copy_ref[rows_of(c), cols] = x
        return carry

    lax.fori_loop(0, n_chunks, scale_rows, 0, unroll=2)


def _mod_body(c_ref, w_ref, b_ref, o_ref):
    s = _silu(c_ref[...]).astype(BF16)
    o_ref[...] = _dot(s, w_ref[...].astype(BF16)) + b_ref[...]


def _mod_call(cvec, w_mod, b_mod, tn=2048):
    depth, d, nout = w_mod.shape
    return pl.pallas_call(
        _mod_body,
        grid=(depth, nout // tn),
        in_specs=[
            pl.BlockSpec((MOD_ROWS, d), lambda l, j: (0, 0)),
            pl.BlockSpec((None, d, tn), lambda l, j: (l, 0, j)),
            pl.BlockSpec((None, 1, tn), lambda l, j: (l, 0, j)),
        ],
        out_specs=pl.BlockSpec((None, MOD_ROWS, tn), lambda l, j: (l, 0, j)),
        out_shape=jax.ShapeDtypeStruct((depth, MOD_ROWS, nout), F32),
        compiler_params=_params("arbitrary", "arbitrary"),
        name="mod",
    )(cvec, w_mod, b_mod.reshape(depth, 1, nout))


def _ffn_body(x_ref, sh_ref, sc_ref, g_ref, nw_ref, wi_hbm, wo_hbm, o_ref,
              h_scr, rs_scr, wg_buf, wu_buf, wo_buf, sem, *, layer, tf, nf, grid):
    tile = pl.program_id(0) * grid[1] + pl.program_id(1)
    n_tiles = grid[0] * grid[1]
    ff = nf * tf

    def weight_copies(f, slot):
        c0 = pl.multiple_of(f * tf, tf)
        c1 = pl.multiple_of(ff + f * tf, tf)
        return (
            pltpu.make_async_copy(wi_hbm.at[layer, :, pl.ds(c0, tf)], wg_buf.at[slot], sem.at[0, slot]),
            pltpu.make_async_copy(wi_hbm.at[layer, :, pl.ds(c1, tf)], wu_buf.at[slot], sem.at[1, slot]),
            pltpu.make_async_copy(wo_hbm.at[layer, pl.ds(c0, tf), :], wo_buf.at[slot], sem.at[2, slot]),
        )

    @pl.when(tile == 0)
    def _():
        for cp in weight_copies(0, 0):
            cp.start()

    _rms_modulate_rows(x_ref, nw_ref, sh_ref, sc_ref, h_scr, rs_scr, copy_ref=o_ref)
    gate = 0.5 * g_ref[...]

    def sub_step(f, slot, f_next, has_next):
        for cp in weight_copies(f, slot):
            cp.wait()

        @pl.when(has_next)
        def _():
            for cp in weight_copies(f_next, 1 - slot):
                cp.start()

        h = h_scr[...]
        g = _dot(h, wg_buf[slot].astype(BF16))
        u = _dot(h, wu_buf[slot].astype(BF16))
        a = (_silu(g) * u).astype(BF16)
        o_ref[...] += gate * _dot(a, wo_buf[slot].astype(BF16))

    def pair(p, carry):
        f0 = 2 * p
        sub_step(f0, 0, f0 + 1, True)
        sub_step(f0 + 1, 1, f0 + 2, True)
        return carry

    lax.fori_loop(0, nf // 2 - 1, pair, 0)
    sub_step(nf - 2, 0, nf - 1, True)
    sub_step(nf - 1, 1, 0, tile + 1 < n_tiles)


def _ffn_call(x, shift, scale, gate, nw, wi, wo, layer, tm=1024, tf=256):
    g_, r_, d = x.shape
    ff = wo.shape[1]
    nf = ff // tf
    assert nf % 2 == 0 and nf * tf == ff
    vec = pl.BlockSpec((None, 1, d), lambda g, i: (g, 0, 0))
    grid = (g_, r_ // tm)
    return pl.pallas_call(
        functools.partial(_ffn_body, layer=layer, tf=tf, nf=nf, grid=grid),
        grid=grid,
        in_specs=[
            pl.BlockSpec((None, tm, d), lambda g, i: (g, i, 0)),
            vec, vec, vec,
            pl.BlockSpec((1, d), lambda g, i: (0, 0)),
            pl.BlockSpec(memory_space=pl.ANY),
            pl.BlockSpec(memory_space=pl.ANY),
        ],
        out_specs=pl.BlockSpec((None, tm, d), lambda g, i: (g, i, 0)),
        out_shape=jax.ShapeDtypeStruct(x.shape, F32),
        scratch_shapes=[
            pltpu.VMEM((tm, d), BF16), pltpu.VMEM((tm, LANES), F32),
            pltpu.VMEM((2, d, tf), F32), pltpu.VMEM((2, d, tf), F32), pltpu.VMEM((2, tf, d), F32),
            pltpu.SemaphoreType.DMA((3, 2)),
        ],
        compiler_params=_params("arbitrary", "arbitrary"),
        name="ffn",
    )(x, shift, scale, gate, nw, wi, wo)


def _proj_in_body(x_ref, sh_ref, sc_ref, nw_ref, w_hbm, gain_ref, o_ref, h_scr, rs_scr, w_buf, sem,
                  *, layer, tn, n_norm_tiles, col_tiles, grid):
    tile = pl.program_id(0) * grid[1] + pl.program_id(1)
    n_tiles = grid[0] * grid[1]
    n_lo, n_hi = col_tiles

    def w_copy(n):
        slot = (n - n_lo) % 2
        return pltpu.make_async_copy(w_hbm.at[layer, :, pl.ds(n * tn, tn)], w_buf.at[slot], sem.at[slot])

    @pl.when(tile == 0)
    def _():
        w_copy(n_lo).start()
        w_copy(n_lo + 1).start()

    _rms_modulate_rows(x_ref, nw_ref, sh_ref, sc_ref, h_scr, rs_scr)

    w_copy(n_lo).wait()
    for n in range(n_lo, n_hi):
        slot = (n - n_lo) % 2
        acc = _dot(h_scr[...], w_buf[slot].astype(BF16))

        if n + 1 < n_hi:
            w_copy(n + 1).wait()
        if n + 2 < n_hi:
            w_copy(n + 2).start()
        else:
            @pl.when(tile + 1 < n_tiles)
            def _():
                w_copy(n + 2 - n_hi + n_lo).start()

        if n < n_norm_tiles:
            for hh in range(tn // HEAD_DIM):
                sl = slice(hh * HEAD_DIM, (hh + 1) * HEAD_DIM)
                gsl = slice(n * tn + hh * HEAD_DIM, n * tn + (hh + 1) * HEAD_DIM)
                osl = slice((n - n_lo) * tn + hh * HEAD_DIM, (n - n_lo) * tn + (hh + 1) * HEAD_DIM)
                t = acc[:, sl]
                y = t * lax.rsqrt(jnp.mean(t * t, axis=-1, keepdims=True) + EPS)
                o_ref[:, osl] = (y * gain_ref[:, gsl]).astype(BF16)
        else:
            o_ref[:, (n - n_lo) * tn:(n - n_lo + 1) * tn] = acc.astype(BF16)


def _proj_in_call(x, shift, scale, nw, w_in, gain, layer, col_range=None, tm=1024, tn=512):
    g_, r_, d = x.shape
    win = w_in.shape[2]
    lo, hi = (0, win) if col_range is None else col_range
    assert lo % tn == 0 and hi % tn == 0 and ((hi - lo) // tn) % 2 == 0 and (2 * NA_WIDTH) % tn == 0
    nout = hi - lo
    grid = (g_, r_ // tm)
    vec = pl.BlockSpec((None, 1, d), lambda g, i: (g, 0, 0))
    body = functools.partial(_proj_in_body, layer=layer, tn=tn, n_norm_tiles=2 * NA_WIDTH // tn,
                             col_tiles=(lo // tn, hi // tn), grid=grid)
    return pl.pallas_call(
        body,
        grid=grid,
        in_specs=[
            pl.BlockSpec((None, tm, d), lambda g, i: (g, i, 0)),
            vec, vec,
            pl.BlockSpec((1, d), lambda g, i: (0, 0)),
            pl.BlockSpec(memory_space=pl.ANY),
            pl.BlockSpec((1, win), lambda g, i: (0, 0)),
        ],
        out_specs=pl.BlockSpec((None, tm, nout), lambda g, i: (g, i, 0)),
        out_shape=jax.ShapeDtypeStruct((g_, r_, nout), BF16),
        scratch_shapes=[pltpu.VMEM((tm, d), BF16), pltpu.VMEM((tm, LANES), F32),
                        pltpu.VMEM((2, d, tn), F32), pltpu.SemaphoreType.DMA((2,))],
        compiler_params=_params("arbitrary", "arbitrary"),
        name="proj_in",
    )(x, shift, scale, nw, w_in, gain)


def _na_plan(rows, kh):
    assert rows % NA_GROUP_ROWS == 0 and rows >= NA_UNION_ROWS and NA_UNION_ROWS % 2 == 0
    assert NA_UNION_ROWS >= NA_GROUP_ROWS + kh - 1
    groups, entries = [], {}
    for r0 in range(0, rows, NA_GROUP_ROWS):
        u0 = min(max(r0 - kh // 2, 0), rows - NA_UNION_ROWS)
        keys = []
        for r in range(r0, r0 + NA_GROUP_ROWS):
            rs = min(max(r - kh // 2, 0), rows - kh)
            assert u0 <= rs and rs + kh <= u0 + NA_UNION_ROWS
            row = []
            for j0 in range(u0, u0 + NA_UNION_ROWS, 2):
                v0, v1 = rs <= j0 < rs + kh, rs <= j0 + 1 < rs + kh
                if v0 and v1:
                    key = ("both", j0 - r + WIN_ROWS - 1)
                elif v0:
                    key = ("lo", j0 - r + WIN_ROWS - 1)
                elif v1:
                    key = ("hi", j0 + 1 - r + WIN_ROWS - 1)
                else:
                    key = None
                if key is not None:
                    entries.setdefault(key, len(entries))
                row.append(key)
            keys.append(row)
        groups.append((r0, u0, keys))
    return groups, entries


def _bias_body(pairs_ref, o_ref, *, entries):
    shape = (GRID_W, 2 * GRID_W)
    n_dc = 2 * WIN_COLS - 1
    cq = lax.broadcasted_iota(jnp.int32, shape, 0)
    lane = lax.broadcasted_iota(jnp.int32, shape, 1)
    ck = lane & (GRID_W - 1)
    hi = lane >= GRID_W
    dc = jnp.clip(ck - cq, -(WIN_COLS - 1), WIN_COLS - 1) + (WIN_COLS - 1)
    idx = dc + jnp.where(hi, n_dc, 0)
    col_start = jnp.clip(cq - WIN_COLS // 2, 0, GRID_W - WIN_COLS)
    col_in = (ck >= col_start) & (ck < col_start + WIN_COLS)
    valid = {"both": col_in, "hi": col_in & hi, "lo": col_in & ~hi}
    for h in range(o_ref.shape[0]):
        for (kind, t), slot in entries.items():
            row = t - 1 if kind == "hi" else t
            src = jnp.broadcast_to(pairs_ref[h, row:row + 1, :], shape)
            tile = jnp.take_along_axis(src, idx, axis=1)
            o_ref[h, slot] = jnp.where(valid[kind], tile, -jnp.inf)


def _bias_call(rpb, entries):
    depth, heads, n_dr, n_dc = rpb.shape
    assert 2 * n_dc <= 2 * GRID_W
    nxt = jnp.concatenate([rpb[:, :, 1:], jnp.zeros_like(rpb[:, :, :1])], axis=2)
    pairs = jnp.concatenate([rpb, nxt, jnp.zeros((depth, heads, n_dr, 2 * GRID_W - 2 * n_dc), F32)], axis=3)
    nslot = len(entries)
    return pl.pallas_call(
        functools.partial(_bias_body, entries=entries),
        grid=(depth,),
        in_specs=[pl.BlockSpec((None, heads, n_dr, 2 * GRID_W), lambda l: (l, 0, 0, 0))],
        out_specs=pl.BlockSpec((heads, nslot, GRID_W, 2 * GRID_W), lambda l: (l, 0, 0, 0)),
        out_shape=jax.ShapeDtypeStruct((depth * heads, nslot, GRID_W, 2 * GRID_W), F32),
        compiler_params=_params("arbitrary"),
        name="bias",
    )(pairs)


def _na_body(q_ref, k_ref, v_ref, kc_ref, vc_ref, bt_ref, o_ref, *, groups, entries):
    kc = kc_ref[...]
    vc = vc_ref[...]
    lc = kc.shape[0]
    pair = 2 * GRID_W
    nwin = NA_UNION_ROWS * GRID_W

    def slices(group):
        r0, u0, _ = group
        return (slice(r0 * GRID_W, (r0 + NA_GROUP_ROWS) * GRID_W),
                slice(u0 * GRID_W, (u0 + NA_UNION_ROWS) * GRID_W))

    def scores(group):
        qs, us = slices(group)
        q = q_ref[qs, :]
        return _dot_nt(q, k_ref[us, :]), _dot_nt(q, kc)

    def finish(group, s_w, s_c):
        qs, us = slices(group)
        inv_l, p_rows = [], []
        for i, row in enumerate(group[2]):
            rsl = slice(i * GRID_W, (i + 1) * GRID_W)
            sc_i = s_c[rsl]
            tiles = {p: s_w[rsl, p * pair:(p + 1) * pair] + bt_ref[entries[key]]
                     for p, key in enumerate(row) if key is not None}
            m_el = sc_i[:, :pair]
            for c0 in range(pair, lc, pair):
                m_el = jnp.maximum(m_el, sc_i[:, c0:c0 + pair])
            for t in tiles.values():
                m_el = jnp.maximum(m_el, t)
            m = jnp.max(m_el, axis=-1, keepdims=True)
            e_c = jnp.exp(sc_i - m)
            l_el = e_c[:, :pair]
            for c0 in range(pair, lc, pair):
                l_el = l_el + e_c[:, c0:c0 + pair]
            parts = []
            for p in range(len(row)):
                if p in tiles:
                    e = jnp.exp(tiles[p] - m)
                    l_el = l_el + e
                    parts.append(e.astype(BF16))
                else:
                    parts.append(jnp.zeros((GRID_W, pair), BF16))
            parts.append(e_c.astype(BF16))
            p_rows.append(jnp.concatenate(parts, axis=1))
            inv_l.append(1.0 / jnp.sum(l_el, axis=-1, keepdims=True))
        p_all = jnp.concatenate(p_rows, axis=0)
        o = _dot(p_all[:, :nwin], v_ref[us, :]) + _dot(p_all[:, nwin:], vc)
        o_ref[qs, :] = (o * jnp.concatenate(inv_l, axis=0)).astype(BF16)

    pending = [scores(g) for g in groups[:NA_SCORE_LOOKAHEAD]]
    for gi, group in enumerate(groups):
        if gi + NA_SCORE_LOOKAHEAD < len(groups):
            pending.append(scores(groups[gi + NA_SCORE_LOOKAHEAD]))
        finish(group, *pending.pop(0))


def _na_call(hl, hc, ctx_k_block, bt, layer, plan):
    b_, n_, _ = hl.shape
    lc = hc.shape[1]
    groups, entries = plan
    nslot = len(entries)
    kofs = NA_WIDTH // HEAD_DIM
    body = functools.partial(_na_body, groups=groups, entries=entries)
    return pl.pallas_call(
        body,
        grid=(b_, N_NA_HEADS),
        in_specs=[
            pl.BlockSpec((None, n_, HEAD_DIM), lambda b, h: (b, 0, h)),
            pl.BlockSpec((None, n_, HEAD_DIM), lambda b, h: (b, 0, kofs + h)),
            pl.BlockSpec((None, n_, HEAD_DIM), lambda b, h: (b, 0, 2 * kofs + h)),
            pl.BlockSpec((None, lc, HEAD_DIM), lambda b, h: (b, 0, ctx_k_block + h)),
            pl.BlockSpec((None, lc, HEAD_DIM), lambda b, h: (b, 0, ctx_k_block + kofs + h)),
            pl.BlockSpec((None, nslot, GRID_W, 2 * GRID_W), lambda b, h: (layer * N_NA_HEADS + h, 0, 0, 0)),
        ],
        out_specs=pl.BlockSpec((None, n_, HEAD_DIM), lambda b, h: (b, 0, h)),
        out_shape=jax.ShapeDtypeStruct((b_, n_, NA_WIDTH), BF16),
        compiler_params=_params("arbitrary", "arbitrary"),
        name="na",
    )(hl, hl, hl, hc, hc, bt)


def _ctx_attn_body(q_ref, k_ref, v_ref, o_ref):
    for h in range(N_NA_HEADS):
        sl = slice(h * HEAD_DIM, (h + 1) * HEAD_DIM)
        s = _dot_nt(q_ref[:, sl], k_ref[:, sl])
        m = jnp.max(s, axis=-1, keepdims=True)
        e = jnp.exp(s - m)
        l = jnp.sum(e, axis=-1, keepdims=True)
        o_ref[:, sl] = (_dot(e.astype(BF16), v_ref[:, sl]) / l).astype(BF16)


def _ctx_attn_call(hc):
    b_, lc, _ = hc.shape
    return pl.pallas_call(
        _ctx_attn_body,
        grid=(b_,),
        in_specs=[
            pl.BlockSpec((None, lc, NA_WIDTH), lambda b: (b, 0, 0)),
            pl.BlockSpec((None, lc, NA_WIDTH), lambda b: (b, 0, 1)),
            pl.BlockSpec((None, lc, NA_WIDTH), lambda b: (b, 0, 2)),
        ],
        out_specs=pl.BlockSpec((None, lc, NA_WIDTH), lambda b: (b, 0, 0)),
        out_shape=jax.ShapeDtypeStruct((b_, lc, NA_WIDTH), BF16),
        compiler_params=_params("arbitrary"),
        name="ctx_attn",
    )(hc, hc, hc)


def _dft_tables(length):
    def cos_sin(freq, n_pos, period):
        ang = ((freq[:, None] * jnp.arange(n_pos, dtype=jnp.int32)[None, :]) % period).astype(F32)
        ang = ang * (2.0 * math.pi / period)
        return jnp.cos(ang), jnp.sin(ang)

    assert length % DFT_ROW_BLOCK == 0
    coarse = jnp.arange(length // DFT_ROW_BLOCK, dtype=jnp.int32) * DFT_ROW_BLOCK
    fine = jnp.arange(DFT_ROW_BLOCK, dtype=jnp.int32)
    ca, sa = cos_sin(coarse, length, length)
    cb, sb = cos_sin(fine, length, length)
    cc, sc = cos_sin(jnp.arange(FOUR_DIM, dtype=jnp.int32), FOUR_DIM, FOUR_DIM)
    scale = 1.0 / math.sqrt(length * FOUR_DIM)
    chan = jnp.concatenate([cc * scale, sc * scale], axis=1).astype(BF16)
    return ca, sa, cb, sb, chan


def _fourier_body(f_ref, chan_ref, ca_ref, sa_ref, cb_ref, sb_ref, wf_ref, o_ref, cl_scr, sl_scr, zc_scr, zs_scr):
    @pl.when(pl.program_id(0) == 0)
    def _():
        cb = cb_ref[...]
        sb = sb_ref[...]
        for a in range(ca_ref.shape[0]):
            rows = slice(a * DFT_ROW_BLOCK, (a + 1) * DFT_ROW_BLOCK)
            ca = ca_ref[a:a + 1, :]
            sa = sa_ref[a:a + 1, :]
            cl_scr[rows, :] = (ca * cb - sa * sb).astype(BF16)
            sl_scr[rows, :] = (-(sa * cb + ca * sb)).astype(BF16)

    for g in range(N_FOUR_GROUPS):
        sl = slice(g * FOUR_DIM, (g + 1) * FOUR_DIM)
        z = _dot(f_ref[:, sl], chan_ref[...])
        zc_scr[:, sl] = z[:, :FOUR_DIM].astype(BF16)
        zs_scr[:, sl] = z[:, FOUR_DIM:].astype(BF16)
    mixed = _dot(cl_scr[...], zc_scr[...]) + _dot(sl_scr[...], zs_scr[...])
    for g in range(N_FOUR_GROUPS):
        sl = slice(g * FOUR_DIM, (g + 1) * FOUR_DIM)
        o_ref[:, sl] = _dot(mixed[:, sl].astype(BF16), wf_ref[g].astype(BF16)).astype(BF16)


def _fourier_call(h, tables, w_four, layer):
    b_, length, _ = h.shape
    ca, sa, cb, sb, chan = tables
    fblk = 3 * NA_WIDTH // FOUR_WIDTH
    whole = lambda arr: pl.BlockSpec(arr.shape, lambda b: (0,) * arr.ndim)
    return pl.pallas_call(
        _fourier_body,
        grid=(b_,),
        in_specs=[
            pl.BlockSpec((None, length, FOUR_WIDTH), lambda b: (b, 0, fblk)),
            whole(chan), whole(ca), whole(sa), whole(cb), whole(sb),
            pl.BlockSpec((None, N_FOUR_GROUPS, FOUR_DIM, FOUR_DIM), lambda b: (layer, 0, 0, 0)),
        ],
        out_specs=pl.BlockSpec((None, length, FOUR_WIDTH), lambda b: (b, 0, 0)),
        out_shape=jax.ShapeDtypeStruct((b_, length, FOUR_WIDTH), BF16),
        scratch_shapes=[pltpu.VMEM((length, length), BF16), pltpu.VMEM((length, length), BF16),
                        pltpu.VMEM((length, FOUR_WIDTH), BF16), pltpu.VMEM((length, FOUR_WIDTH), BF16)],
        compiler_params=_params("arbitrary"),
        name="fourier",
    )(h, chan, ca, sa, cb, sb, w_four)


def _proj_out_body(x_ref, g_ref, na_ref, fo_ref, w_ref, o_ref, wb_scr):
    @pl.when((pl.program_id(0) == 0) & (pl.program_id(1) == 0))
    def _():
        wb_scr[...] = w_ref[...].astype(BF16)

    y = _dot(na_ref[...], wb_scr[:NA_WIDTH, :]) + _dot(fo_ref[...], wb_scr[NA_WIDTH:, :])
    o_ref[...] = x_ref[...] + g_ref[...] * y


def _proj_out_call(x, gate, na, fo, w_out, layer, tm=512):
    g_, r_, d = x.shape
    kin = w_out.shape[1]
    return pl.pallas_call(
        _proj_out_body,
        grid=(g_, r_ // tm),
        in_specs=[
            pl.BlockSpec((None, tm, d), lambda g, i: (g, i, 0)),
            pl.BlockSpec((None, 1, d), lambda g, i: (g, 0, 0)),
            pl.BlockSpec((None, tm, NA_WIDTH), lambda g, i: (g, i, 0)),
            pl.BlockSpec((None, tm, FOUR_WIDTH), lambda g, i: (g, i, 0)),
            pl.BlockSpec((None, kin, d), lambda g, i: (layer, 0, 0), pipeline_mode=pl.Buffered(1)),
        ],
        out_specs=pl.BlockSpec((None, tm, d), lambda g, i: (g, i, 0)),
        out_shape=jax.ShapeDtypeStruct(x.shape, F32),
        scratch_shapes=[pltpu.VMEM((kin, d), BF16)],
        compiler_params=_params("arbitrary", "arbitrary"),
        name="proj_out",
    )(x, gate, na, fo, w_out)


def kernel(x, c, ctx, c_ctx, w_mod, b_mod, norm_w, ffn1_wi, ffn1_wo, w_in, q_norm_w, k_norm_w, rpb,
           w_four, w_out, ffn2_wi, ffn2_wo):
    b_, n_, d = x.shape
    lc = ctx.shape[1]
    depth = w_mod.shape[0]
    rows = n_ // GRID_W

    cvec = jnp.zeros((MOD_ROWS, d), F32).at[:b_].set(c).at[b_].set(c_ctx)
    mod = _mod_call(cvec, w_mod, b_mod).reshape(depth, MOD_ROWS, N_MOD, d)

    plan = _na_plan(rows, min(WIN_ROWS, rows))
    bt = _bias_call(rpb, plan[1])
    tables_lat = _dft_tables(n_)
    tables_ctx = _dft_tables(lc)
    ones = jnp.ones((NA_WIDTH + FOUR_WIDTH,), F32)

    xc = ctx.reshape(1, b_ * lc, d)
    for l in range(depth):
        last = l == depth - 1
        m = [mod[l, :b_, i].reshape(b_, 1, d) for i in range(N_MOD)]
        mc = [mod[l, b_, i].reshape(1, 1, d) for i in range(N_MOD)]
        nw = [norm_w[l, i].reshape(1, d) for i in range(3)]
        gain = jnp.concatenate([
            jnp.tile(q_norm_w[l] * HEAD_DIM ** -0.5, N_NA_HEADS),
            jnp.tile(k_norm_w[l], N_NA_HEADS), ones]).reshape(1, IN_WIDTH)

        x = _ffn_call(x, m[0], m[1], m[2], nw[0], ffn1_wi, ffn1_wo, l)
        xc = _ffn_call(xc, mc[0], mc[1], mc[2], nw[0], ffn1_wi, ffn1_wo, l)

        hl = _proj_in_call(x, m[3], m[4], nw[1], w_in, gain, l)
        ctx_cols = (NA_WIDTH, 3 * NA_WIDTH) if last else (0, IN_WIDTH)
        hc = _proj_in_call(xc, mc[3], mc[4], nw[1], w_in, gain, l, col_range=ctx_cols)
        hc = hc.reshape(b_, lc, ctx_cols[1] - ctx_cols[0])

        na = _na_call(hl, hc, (NA_WIDTH - ctx_cols[0]) // HEAD_DIM, bt, l, plan)
        fo = _fourier_call(hl, tables_lat, w_four, l)
        x = _proj_out_call(x, m[5], na, fo, w_out, l)

        if not last:
            nac = _ctx_attn_call(hc).reshape(1, b_ * lc, NA_WIDTH)
            foc = _fourier_call(hc, tables_ctx, w_four, l).reshape(1, b_ * lc, FOUR_WIDTH)
            xc = _proj_out_call(xc, mc[5], nac, foc, w_out, l)
            xc = _ffn_call(xc, mc[6], mc[7], mc[8], nw[2], ffn2_wi, ffn2_wo, l)

        x = _ffn_call(x, m[6], m[7], m[8], nw[2], ffn2_wi, ffn2_wo, l)
    return x
```

```python
import functools
import math

import jax
import jax.numpy as jnp
from jax import lax
from jax.experimental import pallas as pl
from jax.experimental.pallas import tpu as pltpu

GRID_W = 64
HEAD_DIM = 128
N_NA_HEADS = 12
NA_WIDTH = N_NA_HEADS * HEAD_DIM
N_FOUR_GROUPS = 4
FOUR_DIM = 128
FOUR_WIDTH = N_FOUR_GROUPS * FOUR_DIM
IN_WIDTH = 3 * NA_WIDTH + FOUR_WIDTH
WIN_ROWS = 8
WIN_COLS = 16
N_MOD = 9
EPS = 1e-6

BF16 = jnp.bfloat16
F32 = jnp.float32

V7X_VMEM_LIMIT_BYTES = 56 * 1024 * 1024
MOD_ROWS = 8
RMS_CHUNK_ROWS = 64
LANES = 128
DFT_ROW_BLOCK = 64
NA_GROUP_ROWS = 4
NA_SCORE_LOOKAHEAD = 2
NA_UNION_ROWS = 12


def _params(*sem):
    return pltpu.CompilerParams(dimension_semantics=sem, vmem_limit_bytes=V7X_VMEM_LIMIT_BYTES)


def _dot(a, b):
    return jnp.dot(a, b, preferred_element_type=F32)


def _dot_nt(a, b):
    return lax.dot_general(a, b, (((1,), (1,)), ((), ())), preferred_element_type=F32)


def _silu(x):
    return x / (1.0 + jnp.exp(-x))


def _rms_modulate_rows(x_ref, nw_ref, sh_ref, sc_ref, h_ref, rs_ref, copy_ref=None):
    n_chunks = x_ref.shape[0] // RMS_CHUNK_ROWS

    def rows_of(c):
        return pl.ds(pl.multiple_of(c * RMS_CHUNK_ROWS, RMS_CHUNK_ROWS), RMS_CHUNK_ROWS)

    def row_scale(c, carry):
        x = x_ref[rows_of(c), :]
        rs = lax.rsqrt(jnp.mean(x * x, axis=-1, keepdims=True) + EPS)
        rs_ref[rows_of(c), :] = jnp.broadcast_to(rs, (RMS_CHUNK_ROWS, rs_ref.shape[1]))
        return carry

    lax.fori_loop(0, n_chunks, row_scale, 0, unroll=4)

    def scale_rows(c, carry):
        rs = rs_ref[rows_of(c), :]
        for c0 in range(0, x_ref.shape[1], LANES):
            cols = slice(c0, c0 + LANES)
            x = x_ref[rows_of(c), cols]
            y = ((x * rs) * nw_ref[:, cols]) * (1.0 + sc_ref[:, cols]) + sh_ref[:, cols]
            h_ref[rows_of(c), cols] = y.astype(BF16)
            if copy_ref is not None:
                copy_ref[rows_of(c), cols] = x
        return carry

    lax.fori_loop(0, n_chunks, scale_rows, 0, unroll=2)


def _mod_body(c_ref, w_ref, b_ref, o_ref):
    s = _silu(c_ref[...]).astype(BF16)
    o_ref[...] = _dot(s, w_ref[...].astype(BF16)) + b_ref[...]


def _mod_call(cvec, w_mod, b_mod, tn=2048):
    depth, d, nout = w_mod.shape
    return pl.pallas_call(
        _mod_body,
        grid=(depth, nout // tn),
        in_specs=[
            pl.BlockSpec((MOD_ROWS, d), lambda l, j: (0, 0)),
            pl.BlockSpec((None, d, tn), lambda l, j: (l, 0, j)),
            pl.BlockSpec((None, 1, tn), lambda l, j: (l, 0, j)),
        ],
        out_specs=pl.BlockSpec((None, MOD_ROWS, tn), lambda l, j: (l, 0, j)),
        out_shape=jax.ShapeDtypeStruct((depth, MOD_ROWS, nout), F32),
        compiler_params=_params("arbitrary", "arbitrary"),
        name="mod",
    )(cvec, w_mod, b_mod.reshape(depth, 1, nout))


def _rms_modulate_inline(x_rows, nw_ref, sh_ref, sc_ref, h_rows):
    n_rows, d = x_rows.shape
    for r0 in range(0, n_rows, RMS_CHUNK_ROWS):
        rows = slice(r0, r0 + RMS_CHUNK_ROWS)
        x = x_rows[rows, :]
        rs = lax.rsqrt(jnp.mean(x * x, axis=-1, keepdims=True) + EPS)
        rs = jnp.broadcast_to(rs, (RMS_CHUNK_ROWS, LANES))
        for c0 in range(0, d, LANES):
            cols = slice(c0, c0 + LANES)
            y = ((x_rows[rows, cols] * rs) * nw_ref[:, cols]) * (1.0 + sc_ref[:, cols]) + sh_ref[:, cols]
            h_rows[rows, cols] = y.astype(BF16)


def _ffn_body(x_hbm, sh_ref, sc_ref, g_ref, shn_ref, scn_ref, nw_ref, wi_hbm, wo_hbm, o_ref,
              x_buf, h_scr, rs_scr, wg_buf, wu_buf, wo_buf, sem, xsem, *, layer, tf, nf, grid):
    tm = o_ref.shape[0]
    gi, ti = pl.program_id(0), pl.program_id(1)
    tile = gi * grid[1] + ti
    n_tiles = grid[0] * grid[1]
    slot_t = tile % 2
    has_next = tile + 1 < n_tiles
    ff = nf * tf

    def x_copy(t, slot):
        g = t // grid[1]
        r0 = pl.multiple_of((t - g * grid[1]) * tm, tm)
        return pltpu.make_async_copy(x_hbm.at[g, pl.ds(r0, tm), :], x_buf.at[slot], xsem.at[slot])

    def weight_copies(f, slot):
        c0 = pl.multiple_of(f * tf, tf)
        c1 = pl.multiple_of(ff + f * tf, tf)
        return (
            pltpu.make_async_copy(wi_hbm.at[layer, :, pl.ds(c0, tf)], wg_buf.at[slot], sem.at[0, slot]),
            pltpu.make_async_copy(wi_hbm.at[layer, :, pl.ds(c1, tf)], wu_buf.at[slot], sem.at[1, slot]),
            pltpu.make_async_copy(wo_hbm.at[layer, pl.ds(c0, tf), :], wo_buf.at[slot], sem.at[2, slot]),
        )

    @pl.when(tile == 0)
    def _():
        for cp in weight_copies(0, 0):
            cp.start()
        x_copy(0, 0).start()
        if n_tiles > 1:
            x_copy(1, 1).start()
        x_copy(0, 0).wait()
        _rms_modulate_rows(x_buf.at[0], nw_ref, sh_ref, sc_ref, h_scr.at[0], rs_scr)

    @pl.when(has_next)
    def _():
        x_copy(tile + 1, 1 - slot_t).wait()

    gate = 0.5 * g_ref[...]

    def refill_x():
        @pl.when(tile + 2 < n_tiles)
        def _():
            x_copy(tile + 2, slot_t).start()

    def sub_step(f, slot, f_next, next_pred, side_work=None, first=False):
        for cp in weight_copies(f, slot):
            cp.wait()

        @pl.when(next_pred)
        def _():
            for cp in weight_copies(f_next, 1 - slot):
                cp.start()

        if side_work is not None:
            side_work()
        h = h_scr[slot_t]
        g = _dot(h, wg_buf[slot].astype(BF16))
        u = _dot(h, wu_buf[slot].astype(BF16))
        a = (_silu(g) * u).astype(BF16)
        y = gate * _dot(a, wo_buf[slot].astype(BF16))
        if first:
            o_ref[...] = x_buf[slot_t] + y
        else:
            o_ref[...] += y

    def pair(p, carry):
        f0 = 2 * p
        sub_step(f0, 0, f0 + 1, True)
        sub_step(f0 + 1, 1, f0 + 2, True)
        return carry

    sub_step(0, 0, 1, True, first=True)
    sub_step(1, 1, 2, True, side_work=refill_x)
    lax.fori_loop(1, nf // 2 - 1, pair, 0)
    src = jnp.where(has_next, 1 - slot_t, slot_t)

    def next_h(rows):
        return functools.partial(_rms_modulate_inline, x_buf.at[src, rows, :], nw_ref, shn_ref, scn_ref,
                                 h_scr.at[1 - slot_t, rows, :])

    sub_step(nf - 2, 0, nf - 1, True, side_work=next_h(pl.ds(0, tm // 2)))
    sub_step(nf - 1, 1, 0, has_next, side_work=next_h(pl.ds(tm // 2, tm // 2)))


def _ffn_call(x, shift, scale, gate, nw, wi, wo, layer, tm=1024, tf=256):
    g_, r_, d = x.shape
    ff = wo.shape[1]
    nf = ff // tf
    assert nf % 2 == 0 and nf * tf == ff
    grid = (g_, r_ // tm)
    vec = pl.BlockSpec((None, 1, d), lambda g, i: (g, 0, 0))

    def next_g(g, i):
        return jnp.minimum(g + (i + 1) // grid[1], g_ - 1)

    vec_next = pl.BlockSpec((None, 1, d), lambda g, i: (next_g(g, i), 0, 0))
    return pl.pallas_call(
        functools.partial(_ffn_body, layer=layer, tf=tf, nf=nf, grid=grid),
        grid=grid,
        in_specs=[
            pl.BlockSpec(memory_space=pl.ANY),
            vec, vec, vec, vec_next, vec_next,
            pl.BlockSpec((1, d), lambda g, i: (0, 0)),
            pl.BlockSpec(memory_space=pl.ANY),
            pl.BlockSpec(memory_space=pl.ANY),
        ],
        out_specs=pl.BlockSpec((None, tm, d), lambda g, i: (g, i, 0)),
        out_shape=jax.ShapeDtypeStruct(x.shape, F32),
        scratch_shapes=[
            pltpu.VMEM((2, tm, d), F32), pltpu.VMEM((2, tm, d), BF16), pltpu.VMEM((tm, LANES), F32),
            pltpu.VMEM((2, d, tf), F32), pltpu.VMEM((2, d, tf), F32), pltpu.VMEM((2, tf, d), F32),
            pltpu.SemaphoreType.DMA((3, 2)), pltpu.SemaphoreType.DMA((2,)),
        ],
        compiler_params=_params("arbitrary", "arbitrary"),
        name="ffn",
    )(x, shift, scale, gate, shift, scale, nw, wi, wo)


def _proj_in_body(x_ref, sh_ref, sc_ref, nw_ref, w_hbm, gain_ref, o_ref, h_scr, rs_scr, w_buf, sem,
                  *, layer, tn, n_norm_tiles, col_tiles, grid):
    tile = pl.program_id(0) * grid[1] + pl.program_id(1)
    n_tiles = grid[0] * grid[1]
    n_lo, n_hi = col_tiles

    def w_copy(n):
        slot = (n - n_lo) % 2
        return pltpu.make_async_copy(w_hbm.at[layer, :, pl.ds(n * tn, tn)], w_buf.at[slot], sem.at[slot])

    @pl.when(tile == 0)
    def _():
        w_copy(n_lo).start()
        w_copy(n_lo + 1).start()

    _rms_modulate_rows(x_ref, nw_ref, sh_ref, sc_ref, h_scr, rs_scr)

    w_copy(n_lo).wait()
    for n in range(n_lo, n_hi):
        slot = (n - n_lo) % 2
        acc = _dot(h_scr[...], w_buf[slot].astype(BF16))

        if n + 1 < n_hi:
            w_copy(n + 1).wait()
        if n + 2 < n_hi:
            w_copy(n + 2).start()
        else:
            @pl.when(tile + 1 < n_tiles)
            def _():
                w_copy(n + 2 - n_hi + n_lo).start()

        if n < n_norm_tiles:
            for hh in range(tn // HEAD_DIM):
                sl = slice(hh * HEAD_DIM, (hh + 1) * HEAD_DIM)
                gsl = slice(n * tn + hh * HEAD_DIM, n * tn + (hh + 1) * HEAD_DIM)
                osl = slice((n - n_lo) * tn + hh * HEAD_DIM, (n - n_lo) * tn + (hh + 1) * HEAD_DIM)
                t = acc[:, sl]
                y = t * lax.rsqrt(jnp.mean(t * t, axis=-1, keepdims=True) + EPS)
                o_ref[:, osl] = (y * gain_ref[:, gsl]).astype(BF16)
        else:
            o_ref[:, (n - n_lo) * tn:(n - n_lo + 1) * tn] = acc.astype(BF16)


def _proj_in_call(x, shift, scale, nw, w_in, gain, layer, col_range=None, tm=1024, tn=512):
    g_, r_, d = x.shape
    win = w_in.shape[2]
    lo, hi = (0, win) if col_range is None else col_range
    assert lo % tn == 0 and hi % tn == 0 and ((hi - lo) // tn) % 2 == 0 and (2 * NA_WIDTH) % tn == 0
    nout = hi - lo
    grid = (g_, r_ // tm)
    vec = pl.BlockSpec((None, 1, d), lambda g, i: (g, 0, 0))
    body = functools.partial(_proj_in_body, layer=layer, tn=tn, n_norm_tiles=2 * NA_WIDTH // tn,
                             col_tiles=(lo // tn, hi // tn), grid=grid)
    return pl.pallas_call(
        body,
        grid=grid,
        in_specs=[
            pl.BlockSpec((None, tm, d), lambda g, i: (g, i, 0)),
            vec, vec,
            pl.BlockSpec((1, d), lambda g, i: (0, 0)),
            pl.BlockSpec(memory_space=pl.ANY),
            pl.BlockSpec((1, win), lambda g, i: (0, 0)),
        ],
        out_specs=pl.BlockSpec((None, tm, nout), lambda g, i: (g, i, 0)),
        out_shape=jax.ShapeDtypeStruct((g_, r_, nout), BF16),
        scratch_shapes=[pltpu.VMEM((tm, d), BF16), pltpu.VMEM((tm, LANES), F32),
                        pltpu.VMEM((2, d, tn), F32), pltpu.SemaphoreType.DMA((2,))],
        compiler_params=_params("arbitrary", "arbitrary"),
        name="proj_in",
    )(x, shift, scale, nw, w_in, gain)


def _na_plan(rows, kh):
    assert rows % NA_GROUP_ROWS == 0 and rows >= NA_UNION_ROWS and NA_UNION_ROWS % 2 == 0
    assert NA_UNION_ROWS >= NA_GROUP_ROWS + kh - 1
    groups, entries = [], {}
    for r0 in range(0, rows, NA_GROUP_ROWS):
        u0 = min(max(r0 - kh // 2, 0), rows - NA_UNION_ROWS)
        keys = []
        for r in range(r0, r0 + NA_GROUP_ROWS):
            rs = min(max(r - kh // 2, 0), rows - kh)
            assert u0 <= rs and rs + kh <= u0 + NA_UNION_ROWS
            row = []
            for j0 in range(u0, u0 + NA_UNION_ROWS, 2):
                v0, v1 = rs <= j0 < rs + kh, rs <= j0 + 1 < rs + kh
                if v0 and v1:
                    key = ("both", j0 - r + WIN_ROWS - 1)
                elif v0:
                    key = ("lo", j0 - r + WIN_ROWS - 1)
                elif v1:
                    key = ("hi", j0 + 1 - r + WIN_ROWS - 1)
                else:
                    key = None
                if key is not None:
                    entries.setdefault(key, len(entries))
                row.append(key)
            keys.append(row)
        groups.append((r0, u0, keys))
    return groups, entries


def _bias_body(pairs_ref, o_ref, *, entries):
    shape = (GRID_W, 2 * GRID_W)
    n_dc = 2 * WIN_COLS - 1
    cq = lax.broadcasted_iota(jnp.int32, shape, 0)
    lane = lax.broadcasted_iota(jnp.int32, shape, 1)
    ck = lane & (GRID_W - 1)
    hi = lane >= GRID_W
    dc = jnp.clip(ck - cq, -(WIN_COLS - 1), WIN_COLS - 1) + (WIN_COLS - 1)
    idx = dc + jnp.where(hi, n_dc, 0)
    col_start = jnp.clip(cq - WIN_COLS // 2, 0, GRID_W - WIN_COLS)
    col_in = (ck >= col_start) & (ck < col_start + WIN_COLS)
    valid = {"both": col_in, "hi": col_in & hi, "lo": col_in & ~hi}
    for h in range(o_ref.shape[0]):
        for (kind, t), slot in entries.items():
            row = t - 1 if kind == "hi" else t
            src = jnp.broadcast_to(pairs_ref[h, row:row + 1, :], shape)
            tile = jnp.take_along_axis(src, idx, axis=1)
            o_ref[h, slot] = jnp.where(valid[kind], tile, -jnp.inf)


def _bias_call(rpb, entries):
    depth, heads, n_dr, n_dc = rpb.shape
    assert 2 * n_dc <= 2 * GRID_W
    nxt = jnp.concatenate([rpb[:, :, 1:], jnp.zeros_like(rpb[:, :, :1])], axis=2)
    pairs = jnp.concatenate([rpb, nxt, jnp.zeros((depth, heads, n_dr, 2 * GRID_W - 2 * n_dc), F32)], axis=3)
    nslot = len(entries)
    return pl.pallas_call(
        functools.partial(_bias_body, entries=entries),
        grid=(depth,),
        in_specs=[pl.BlockSpec((None, heads, n_dr, 2 * GRID_W), lambda l: (l, 0, 0, 0))],
        out_specs=pl.BlockSpec((heads, nslot, GRID_W, 2 * GRID_W), lambda l: (l, 0, 0, 0)),
        out_shape=jax.ShapeDtypeStruct((depth * heads, nslot, GRID_W, 2 * GRID_W), F32),
        compiler_params=_params("arbitrary"),
        name="bias",
    )(pairs)


def _na_body(q_ref, k_ref, v_ref, kc_ref, vc_ref, bt_ref, o_ref, *, groups, entries):
    kc = kc_ref[...]
    vc = vc_ref[...]
    lc = kc.shape[0]
    pair = 2 * GRID_W
    nwin = NA_UNION_ROWS * GRID_W

    def slices(group):
        r0, u0, _ = group
        return (slice(r0 * GRID_W, (r0 + NA_GROUP_ROWS) * GRID_W),
                slice(u0 * GRID_W, (u0 + NA_UNION_ROWS) * GRID_W))

    def scores(group):
        qs, us = slices(group)
        q = q_ref[qs, :]
        return _dot_nt(q, k_ref[us, :]), _dot_nt(q, kc)

    def finish(group, s_w, s_c):
        qs, us = slices(group)
        inv_l, p_rows = [], []
        for i, row in enumerate(group[2]):
            rsl = slice(i * GRID_W, (i + 1) * GRID_W)
            sc_i = s_c[rsl]
            tiles = {p: s_w[rsl, p * pair:(p + 1) * pair] + bt_ref[entries[key]]
                     for p, key in enumerate(row) if key is not None}
            m_el = sc_i[:, :pair]
            for c0 in range(pair, lc, pair):
                m_el = jnp.maximum(m_el, sc_i[:, c0:c0 + pair])
            for t in tiles.values():
                m_el = jnp.maximum(m_el, t)
            m = jnp.max(m_el, axis=-1, keepdims=True)
            e_c = jnp.exp(sc_i - m)
            l_el = e_c[:, :pair]
            for c0 in range(pair, lc, pair):
                l_el = l_el + e_c[:, c0:c0 + pair]
            parts = []
            for p in range(len(row)):
                if p in tiles:
                    e = jnp.exp(tiles[p] - m)
                    l_el = l_el + e
                    parts.append(e.astype(BF16))
                else:
                    parts.append(jnp.zeros((GRID_W, pair), BF16))
            parts.append(e_c.astype(BF16))
            p_rows.append(jnp.concatenate(parts, axis=1))
            inv_l.append(1.0 / jnp.sum(l_el, axis=-1, keepdims=True))
        p_all = jnp.concatenate(p_rows, axis=0)
        o = _dot(p_all[:, :nwin], v_ref[us, :]) + _dot(p_all[:, nwin:], vc)
        o_ref[qs, :] = (o * jnp.concatenate(inv_l, axis=0)).astype(BF16)

    pending = [scores(g) for g in groups[:NA_SCORE_LOOKAHEAD]]
    for gi, group in enumerate(groups):
        if gi + NA_SCORE_LOOKAHEAD < len(groups):
            pending.append(scores(groups[gi + NA_SCORE_LOOKAHEAD]))
        finish(group, *pending.pop(0))


def _na_call(hl, hc, ctx_k_block, bt, layer, plan):
    b_, n_, _ = hl.shape
    lc = hc.shape[1]
    groups, entries = plan
    nslot = len(entries)
    kofs = NA_WIDTH // HEAD_DIM
    body = functools.partial(_na_body, groups=groups, entries=entries)
    return pl.pallas_call(
        body,
        grid=(b_, N_NA_HEADS),
        in_specs=[
            pl.BlockSpec((None, n_, HEAD_DIM), lambda b, h: (b, 0, h)),
            pl.BlockSpec((None, n_, HEAD_DIM), lambda b, h: (b, 0, kofs + h)),
            pl.BlockSpec((None, n_, HEAD_DIM), lambda b, h: (b, 0, 2 * kofs + h)),
            pl.BlockSpec((None, lc, HEAD_DIM), lambda b, h: (b, 0, ctx_k_block + h)),
            pl.BlockSpec((None, lc, HEAD_DIM), lambda b, h: (b, 0, ctx_k_block + kofs + h)),
            pl.BlockSpec((None, nslot, GRID_W, 2 * GRID_W), lambda b, h: (layer * N_NA_HEADS + h, 0, 0, 0)),
        ],
        out_specs=pl.BlockSpec((None, n_, HEAD_DIM), lambda b, h: (b, 0, h)),
        out_shape=jax.ShapeDtypeStruct((b_, n_, NA_WIDTH), BF16),
        compiler_params=_params("arbitrary", "arbitrary"),
        name="na",
    )(hl, hl, hl, hc, hc, bt)


def _ctx_attn_body(q_ref, k_ref, v_ref, o_ref):
    for h in range(N_NA_HEADS):
        sl = slice(h * HEAD_DIM, (h + 1) * HEAD_DIM)
        s = _dot_nt(q_ref[:, sl], k_ref[:, sl])
        m = jnp.max(s, axis=-1, keepdims=True)
        e = jnp.exp(s - m)
        l = jnp.sum(e, axis=-1, keepdims=True)
        o_ref[:, sl] = (_dot(e.astype(BF16), v_ref[:, sl]) / l).astype(BF16)


def _ctx_attn_call(hc):
    b_, lc, _ = hc.shape
    return pl.pallas_call(
        _ctx_attn_body,
        grid=(b_,),
        in_specs=[
            pl.BlockSpec((None, lc, NA_WIDTH), lambda b: (b, 0, 0)),
            pl.BlockSpec((None, lc, NA_WIDTH), lambda b: (b, 0, 1)),
            pl.BlockSpec((None, lc, NA_WIDTH), lambda b: (b, 0, 2)),
        ],
        out_specs=pl.BlockSpec((None, lc, NA_WIDTH), lambda b: (b, 0, 0)),
        out_shape=jax.ShapeDtypeStruct((b_, lc, NA_WIDTH), BF16),
        compiler_params=_params("arbitrary"),
        name="ctx_attn",
    )(hc, hc, hc)


def _dft_tables(length):
    def cos_sin(freq, n_pos, period):
        ang = ((freq[:, None] * jnp.arange(n_pos, dtype=jnp.int32)[None, :]) % period).astype(F32)
        ang = ang * (2.0 * math.pi / period)
        return jnp.cos(ang), jnp.sin(ang)

    assert length % DFT_ROW_BLOCK == 0
    coarse = jnp.arange(length // DFT_ROW_BLOCK, dtype=jnp.int32) * DFT_ROW_BLOCK
    fine = jnp.arange(DFT_ROW_BLOCK, dtype=jnp.int32)
    ca, sa = cos_sin(coarse, length, length)
    cb, sb = cos_sin(fine, length, length)
    cc, sc = cos_sin(jnp.arange(FOUR_DIM, dtype=jnp.int32), FOUR_DIM, FOUR_DIM)
    scale = 1.0 / math.sqrt(length * FOUR_DIM)
    chan = jnp.concatenate([cc * scale, sc * scale], axis=1).astype(BF16)
    return ca, sa, cb, sb, chan


def _fourier_body(f_ref, chan_ref, ca_ref, sa_ref, cb_ref, sb_ref, wf_ref, o_ref, cl_scr, sl_scr, zc_scr, zs_scr):
    @pl.when(pl.program_id(0) == 0)
    def _():
        cb = cb_ref[...]
        sb = sb_ref[...]
        for a in range(ca_ref.shape[0]):
            rows = slice(a * DFT_ROW_BLOCK, (a + 1) * DFT_ROW_BLOCK)
            ca = ca_ref[a:a + 1, :]
            sa = sa_ref[a:a + 1, :]
            cl_scr[rows, :] = (ca * cb - sa * sb).astype(BF16)
            sl_scr[rows, :] = (-(sa * cb + ca * sb)).astype(BF16)

    for g in range(N_FOUR_GROUPS):
        sl = slice(g * FOUR_DIM, (g + 1) * FOUR_DIM)
        z = _dot(f_ref[:, sl], chan_ref[...])
        zc_scr[:, sl] = z[:, :FOUR_DIM].astype(BF16)
        zs_scr[:, sl] = z[:, FOUR_DIM:].astype(BF16)
    mixed = _dot(cl_scr[...], zc_scr[...]) + _dot(sl_scr[...], zs_scr[...])
    for g in range(N_FOUR_GROUPS):
        sl = slice(g * FOUR_DIM, (g + 1) * FOUR_DIM)
        o_ref[:, sl] = _dot(mixed[:, sl].astype(BF16), wf_ref[g].astype(BF16)).astype(BF16)


def _fourier_call(h, tables, w_four, layer):
    b_, length, _ = h.shape
    ca, sa, cb, sb, chan = tables
    fblk = 3 * NA_WIDTH // FOUR_WIDTH
    whole = lambda arr: pl.BlockSpec(arr.shape, lambda b: (0,) * arr.ndim)
    return pl.pallas_call(
        _fourier_body,
        grid=(b_,),
        in_specs=[
            pl.BlockSpec((None, length, FOUR_WIDTH), lambda b: (b, 0, fblk)),
            whole(chan), whole(ca), whole(sa), whole(cb), whole(sb),
            pl.BlockSpec((None, N_FOUR_GROUPS, FOUR_DIM, FOUR_DIM), lambda b: (layer, 0, 0, 0)),
        ],
        out_specs=pl.BlockSpec((None, length, FOUR_WIDTH), lambda b: (b, 0, 0)),
        out_shape=jax.ShapeDtypeStruct((b_, length, FOUR_WIDTH), BF16),
        scratch_shapes=[pltpu.VMEM((length, length), BF16), pltpu.VMEM((length, length), BF16),
                        pltpu.VMEM((length, FOUR_WIDTH), BF16), pltpu.VMEM((length, FOUR_WIDTH), BF16)],
        compiler_params=_params("arbitrary"),
        name="fourier",
    )(h, chan, ca, sa, cb, sb, w_four)


def _proj_out_body(x_ref, g_ref, na_ref, fo_ref, w_ref, o_ref, wb_scr):
    @pl.when((pl.program_id(0) == 0) & (pl.program_id(1) == 0))
    def _():
        wb_scr[...] = w_ref[...].astype(BF16)

    y = _dot(na_ref[...], wb_scr[:NA_WIDTH, :]) + _dot(fo_ref[...], wb_scr[NA_WIDTH:, :])
    o_ref[...] = x_ref[...] + g_ref[...] * y


def _proj_out_call(x, gate, na, fo, w_out, layer, tm=512):
    g_, r_, d = x.shape
    kin = w_out.shape[1]
    return pl.pallas_call(
        _proj_out_body,
        grid=(g_, r_ // tm),
        in_specs=[
            pl.BlockSpec((None, tm, d), lambda g, i: (g, i, 0)),
            pl.BlockSpec((None, 1, d), lambda g, i: (g, 0, 0)),
            pl.BlockSpec((None, tm, NA_WIDTH), lambda g, i: (g, i, 0)),
            pl.BlockSpec((None, tm, FOUR_WIDTH), lambda g, i: (g, i, 0)),
            pl.BlockSpec((None, kin, d), lambda g, i: (layer, 0, 0), pipeline_mode=pl.Buffered(1)),
        ],
        out_specs=pl.BlockSpec((None, tm, d), lambda g, i: (g, i, 0)),
        out_shape=jax.ShapeDtypeStruct(x.shape, F32),
        scratch_shapes=[pltpu.VMEM((kin, d), BF16)],
        compiler_params=_params("arbitrary", "arbitrary"),
        name="proj_out",
    )(x, gate, na, fo, w_out)


def kernel(x, c, ctx, c_ctx, w_mod, b_mod, norm_w, ffn1_wi, ffn1_wo, w_in, q_norm_w, k_norm_w, rpb,
           w_four, w_out, ffn2_wi, ffn2_wo):
    b_, n_, d = x.shape
    lc = ctx.shape[1]
    depth = w_mod.shape[0]
    rows = n_ // GRID_W

    cvec = jnp.zeros((MOD_ROWS, d), F32).at[:b_].set(c).at[b_].set(c_ctx)
    mod = _mod_call(cvec, w_mod, b_mod).reshape(depth, MOD_ROWS, N_MOD, d)

    plan = _na_plan(rows, min(WIN_ROWS, rows))
    bt = _bias_call(rpb, plan[1])
    tables_lat = _dft_tables(n_)
    tables_ctx = _dft_tables(lc)
    ones = jnp.ones((NA_WIDTH + FOUR_WIDTH,), F32)

    xc = ctx.reshape(1, b_ * lc, d)
    for l in range(depth):
        last = l == depth - 1
        m = [mod[l, :b_, i].reshape(b_, 1, d) for i in range(N_MOD)]
        mc = [mod[l, b_, i].reshape(1, 1, d) for i in range(N_MOD)]
        nw = [norm_w[l, i].reshape(1, d) for i in range(3)]
        gain = jnp.concatenate([
            jnp.tile(q_norm_w[l] * HEAD_DIM ** -0.5, N_NA_HEADS),
            jnp.tile(k_norm_w[l], N_NA_HEADS), ones]).reshape(1, IN_WIDTH)

        x = _ffn_call(x, m[0], m[1], m[2], nw[0], ffn1_wi, ffn1_wo, l)
        xc = _ffn_call(xc, mc[0], mc[1], mc[2], nw[0], ffn1_wi, ffn1_wo, l)

        hl = _proj_in_call(x, m[3], m[4], nw[1], w_in, gain, l)
        ctx_cols = (NA_WIDTH, 3 * NA_WIDTH) if last else (0, IN_WIDTH)
        hc = _proj_in_call(xc, mc[3], mc[4], nw[1], w_in, gain, l, col_range=ctx_cols)
        hc = hc.reshape(b_, lc, ctx_cols[1] - ctx_cols[0])

        na = _na_call(hl, hc, (NA_WIDTH - ctx_cols[0]) // HEAD_DIM, bt, l, plan)
        fo = _fourier_call(hl, tables_lat, w_four, l)
        x = _proj_out_call(x, m[5], na, fo, w_out, l)

        if not last:
            nac = _ctx_attn_call(hc).reshape(1, b_ * lc, NA_WIDTH)
            foc = _fourier_call(hc, tables_ctx, w_four, l).reshape(1, b_ * lc, FOUR_WIDTH)
            xc = _proj_out_call(xc, mc[5], nac, foc, w_out, l)
            xc = _ffn_call(xc, mc[6], mc[7], mc[8], nw[2], ffn2_wi, ffn2_wo, l)

        x = _ffn_call(x, m[6], m[7], m[8], nw[2], ffn2_wi, ffn2_wo, l)
    return x
```

```python
import functools
import math

import jax
import jax.numpy as jnp
from jax import lax
from jax.experimental import pallas as pl
from jax.experimental.pallas import tpu as pltpu

GRID_W = 64
HEAD_DIM = 128
N_NA_HEADS = 12
NA_WIDTH = N_NA_HEADS * HEAD_DIM
N_FOUR_GROUPS = 4
FOUR_DIM = 128
FOUR_WIDTH = N_FOUR_GROUPS * FOUR_DIM
IN_WIDTH = 3 * NA_WIDTH + FOUR_WIDTH
WIN_ROWS = 8
WIN_COLS = 16
N_MOD = 9
EPS = 1e-6

BF16 = jnp.bfloat16
F32 = jnp.float32

V7X_VMEM_LIMIT_BYTES = 56 * 1024 * 1024
MOD_ROWS = 8
RMS_CHUNK_ROWS = 64
LANES = 128
DFT_ROW_BLOCK = 64
NA_GROUP_ROWS = 4
NA_HEADS_PER_STEP = 4
NA_SCORE_LOOKAHEAD = 2
NA_UNION_ROWS = 12


def _params(*sem):
    return pltpu.CompilerParams(dimension_semantics=sem, vmem_limit_bytes=V7X_VMEM_LIMIT_BYTES)


def _dot(a, b):
    return jnp.dot(a, b, preferred_element_type=F32)


def _dot_nt(a, b):
    return lax.dot_general(a, b, (((1,), (1,)), ((), ())), preferred_element_type=F32)


def _silu(x):
    return x / (1.0 + jnp.exp(-x))


def _rms_modulate_rows(x_ref, nw_ref, sh_ref, sc_ref, h_ref, rs_ref, copy_ref=None):
    n_chunks = x_ref.shape[0] // RMS_CHUNK_ROWS

    def rows_of(c):
        return pl.ds(pl.multiple_of(c * RMS_CHUNK_ROWS, RMS_CHUNK_ROWS), RMS_CHUNK_ROWS)

    def row_scale(c, carry):
        x = x_ref[rows_of(c), :]
        rs = lax.rsqrt(jnp.mean(x * x, axis=-1, keepdims=True) + EPS)
        rs_ref[rows_of(c), :] = jnp.broadcast_to(rs, (RMS_CHUNK_ROWS, rs_ref.shape[1]))
        return carry

    lax.fori_loop(0, n_chunks, row_scale, 0, unroll=4)

    def scale_rows(c, carry):
        rs = rs_ref[rows_of(c), :]
        for c0 in range(0, x_ref.shape[1], LANES):
            cols = slice(c0, c0 + LANES)
            x = x_ref[rows_of(c), cols]
            y = ((x * rs) * nw_ref[:, cols]) * (1.0 + sc_ref[:, cols]) + sh_ref[:, cols]
            h_ref[rows_of(c), cols] = y.astype(BF16)
            if copy_ref is not None:
                copy_ref[rows_of(c), cols] = x
        return carry

    lax.fori_loop(0, n_chunks, scale_rows, 0, unroll=2)


def _mod_body(c_ref, w_ref, b_ref, o_ref):
    s = _silu(c_ref[...]).astype(BF16)
    o_ref[...] = _dot(s, w_ref[...].astype(BF16)) + b_ref[...]


def _mod_call(cvec, w_mod, b_mod, tn=2048):
    depth, d, nout = w_mod.shape
    return pl.pallas_call(
        _mod_body,
        grid=(depth, nout // tn),
        in_specs=[
            pl.BlockSpec((MOD_ROWS, d), lambda l, j: (0, 0)),
            pl.BlockSpec((None, d, tn), lambda l, j: (l, 0, j)),
            pl.BlockSpec((None, 1, tn), lambda l, j: (l, 0, j)),
        ],
        out_specs=pl.BlockSpec((None, MOD_ROWS, tn), lambda l, j: (l, 0, j)),
        out_shape=jax.ShapeDtypeStruct((depth, MOD_ROWS, nout), F32),
        compiler_params=_params("arbitrary", "arbitrary"),
        name="mod",
    )(cvec, w_mod, b_mod.reshape(depth, 1, nout))


def _ffn_body(x_ref, sh_ref, sc_ref, g_ref, nw_ref, wi_hbm, wo_hbm, o_ref,
              h_scr, rs_scr, wg_buf, wu_buf, wo_buf, sem, *, layer, tf, nf, grid):
    tile = pl.program_id(0) * grid[1] + pl.program_id(1)
    n_tiles = grid[0] * grid[1]
    ff = nf * tf

    def weight_copies(f, slot):
        c0 = pl.multiple_of(f * tf, tf)
        c1 = pl.multiple_of(ff + f * tf, tf)
        return (
            pltpu.make_async_copy(wi_hbm.at[layer, :, pl.ds(c0, tf)], wg_buf.at[slot], sem.at[0, slot]),
            pltpu.make_async_copy(wi_hbm.at[layer, :, pl.ds(c1, tf)], wu_buf.at[slot], sem.at[1, slot]),
            pltpu.make_async_copy(wo_hbm.at[layer, pl.ds(c0, tf), :], wo_buf.at[slot], sem.at[2, slot]),
        )

    @pl.when(tile == 0)
    def _():
        for cp in weight_copies(0, 0):
            cp.start()

    _rms_modulate_rows(x_ref, nw_ref, sh_ref, sc_ref, h_scr, rs_scr, copy_ref=o_ref)
    gate = 0.5 * g_ref[...]

    def sub_step(f, slot, f_next, has_next):
        for cp in weight_copies(f, slot):
            cp.wait()

        @pl.when(has_next)
        def _():
            for cp in weight_copies(f_next, 1 - slot):
                cp.start()

        h = h_scr[...]
        g = _dot(h, wg_buf[slot].astype(BF16))
        u = _dot(h, wu_buf[slot].astype(BF16))
        a = (_silu(g) * u).astype(BF16)
        o_ref[...] += gate * _dot(a, wo_buf[slot].astype(BF16))

    def pair(p, carry):
        f0 = 2 * p
        sub_step(f0, 0, f0 + 1, True)
        sub_step(f0 + 1, 1, f0 + 2, True)
        return carry

    lax.fori_loop(0, nf // 2 - 1, pair, 0)
    sub_step(nf - 2, 0, nf - 1, True)
    sub_step(nf - 1, 1, 0, tile + 1 < n_tiles)


def _ffn_call(x, shift, scale, gate, nw, wi, wo, layer, tm=1024, tf=256):
    g_, r_, d = x.shape
    ff = wo.shape[1]
    nf = ff // tf
    assert nf % 2 == 0 and nf * tf == ff
    vec = pl.BlockSpec((None, 1, d), lambda g, i: (g, 0, 0))
    grid = (g_, r_ // tm)
    return pl.pallas_call(
        functools.partial(_ffn_body, layer=layer, tf=tf, nf=nf, grid=grid),
        grid=grid,
        in_specs=[
            pl.BlockSpec((None, tm, d), lambda g, i: (g, i, 0)),
            vec, vec, vec,
            pl.BlockSpec((1, d), lambda g, i: (0, 0)),
            pl.BlockSpec(memory_space=pl.ANY),
            pl.BlockSpec(memory_space=pl.ANY),
        ],
        out_specs=pl.BlockSpec((None, tm, d), lambda g, i: (g, i, 0)),
        out_shape=jax.ShapeDtypeStruct(x.shape, F32),
        scratch_shapes=[
            pltpu.VMEM((tm, d), BF16), pltpu.VMEM((tm, LANES), F32),
            pltpu.VMEM((2, d, tf), F32), pltpu.VMEM((2, d, tf), F32), pltpu.VMEM((2, tf, d), F32),
            pltpu.SemaphoreType.DMA((3, 2)),
        ],
        compiler_params=_params("arbitrary", "arbitrary"),
        name="ffn",
    )(x, shift, scale, gate, nw, wi, wo)


def _proj_in_body(x_ref, sh_ref, sc_ref, nw_ref, w_hbm, gain_ref, o_ref, h_scr, rs_scr, w_buf, sem,
                  *, layer, tn, n_norm_tiles, col_tiles, grid):
    tile = pl.program_id(0) * grid[1] + pl.program_id(1)
    n_tiles = grid[0] * grid[1]
    n_lo, n_hi = col_tiles

    def w_copy(n):
        slot = (n - n_lo) % 2
        return pltpu.make_async_copy(w_hbm.at[layer, :, pl.ds(n * tn, tn)], w_buf.at[slot], sem.at[slot])

    @pl.when(tile == 0)
    def _():
        w_copy(n_lo).start()
        w_copy(n_lo + 1).start()

    _rms_modulate_rows(x_ref, nw_ref, sh_ref, sc_ref, h_scr, rs_scr)

    w_copy(n_lo).wait()
    for n in range(n_lo, n_hi):
        slot = (n - n_lo) % 2
        acc = _dot(h_scr[...], w_buf[slot].astype(BF16))

        if n + 1 < n_hi:
            w_copy(n + 1).wait()
        if n + 2 < n_hi:
            w_copy(n + 2).start()
        else:
            @pl.when(tile + 1 < n_tiles)
            def _():
                w_copy(n + 2 - n_hi + n_lo).start()

        if n < n_norm_tiles:
            for hh in range(tn // HEAD_DIM):
                sl = slice(hh * HEAD_DIM, (hh + 1) * HEAD_DIM)
                gsl = slice(n * tn + hh * HEAD_DIM, n * tn + (hh + 1) * HEAD_DIM)
                osl = slice((n - n_lo) * tn + hh * HEAD_DIM, (n - n_lo) * tn + (hh + 1) * HEAD_DIM)
                t = acc[:, sl]
                y = t * lax.rsqrt(jnp.mean(t * t, axis=-1, keepdims=True) + EPS)
                o_ref[:, osl] = (y * gain_ref[:, gsl]).astype(BF16)
        else:
            o_ref[:, (n - n_lo) * tn:(n - n_lo + 1) * tn] = acc.astype(BF16)


def _proj_in_call(x, shift, scale, nw, w_in, gain, layer, col_range=None, tm=1024, tn=512):
    g_, r_, d = x.shape
    win = w_in.shape[2]
    lo, hi = (0, win) if col_range is None else col_range
    assert lo % tn == 0 and hi % tn == 0 and ((hi - lo) // tn) % 2 == 0 and (2 * NA_WIDTH) % tn == 0
    nout = hi - lo
    grid = (g_, r_ // tm)
    vec = pl.BlockSpec((None, 1, d), lambda g, i: (g, 0, 0))
    body = functools.partial(_proj_in_body, layer=layer, tn=tn, n_norm_tiles=2 * NA_WIDTH // tn,
                             col_tiles=(lo // tn, hi // tn), grid=grid)
    return pl.pallas_call(
        body,
        grid=grid,
        in_specs=[
            pl.BlockSpec((None, tm, d), lambda g, i: (g, i, 0)),
            vec, vec,
            pl.BlockSpec((1, d), lambda g, i: (0, 0)),
            pl.BlockSpec(memory_space=pl.ANY),
            pl.BlockSpec((1, win), lambda g, i: (0, 0)),
        ],
        out_specs=pl.BlockSpec((None, tm, nout), lambda g, i: (g, i, 0)),
        out_shape=jax.ShapeDtypeStruct((g_, r_, nout), BF16),
        scratch_shapes=[pltpu.VMEM((tm, d), BF16), pltpu.VMEM((tm, LANES), F32),
                        pltpu.VMEM((2, d, tn), F32), pltpu.SemaphoreType.DMA((2,))],
        compiler_params=_params("arbitrary", "arbitrary"),
        name="proj_in",
    )(x, shift, scale, nw, w_in, gain)


def _na_plan(rows, kh):
    assert rows % NA_GROUP_ROWS == 0 and rows >= NA_UNION_ROWS and NA_UNION_ROWS % 2 == 0
    assert NA_UNION_ROWS >= NA_GROUP_ROWS + kh - 1
    groups, entries = [], {}
    for r0 in range(0, rows, NA_GROUP_ROWS):
        u0 = min(max(r0 - kh // 2, 0), rows - NA_UNION_ROWS)
        keys = []
        for r in range(r0, r0 + NA_GROUP_ROWS):
            rs = min(max(r - kh // 2, 0), rows - kh)
            assert u0 <= rs and rs + kh <= u0 + NA_UNION_ROWS
            row = []
            for j0 in range(u0, u0 + NA_UNION_ROWS, 2):
                v0, v1 = rs <= j0 < rs + kh, rs <= j0 + 1 < rs + kh
                if v0 and v1:
                    key = ("both", j0 - r + WIN_ROWS - 1)
                elif v0:
                    key = ("lo", j0 - r + WIN_ROWS - 1)
                elif v1:
                    key = ("hi", j0 + 1 - r + WIN_ROWS - 1)
                else:
                    key = None
                if key is not None:
                    entries.setdefault(key, len(entries))
                row.append(key)
            keys.append(row)
        groups.append((r0, u0, keys))
    return groups, entries


def _bias_body(pairs_ref, o_ref, *, entries):
    shape = (GRID_W, 2 * GRID_W)
    n_dc = 2 * WIN_COLS - 1
    cq = lax.broadcasted_iota(jnp.int32, shape, 0)
    lane = lax.broadcasted_iota(jnp.int32, shape, 1)
    ck = lane & (GRID_W - 1)
    hi = lane >= GRID_W
    dc = jnp.clip(ck - cq, -(WIN_COLS - 1), WIN_COLS - 1) + (WIN_COLS - 1)
    idx = dc + jnp.where(hi, n_dc, 0)
    col_start = jnp.clip(cq - WIN_COLS // 2, 0, GRID_W - WIN_COLS)
    col_in = (ck >= col_start) & (ck < col_start + WIN_COLS)
    valid = {"both": col_in, "hi": col_in & hi, "lo": col_in & ~hi}
    for h in range(o_ref.shape[0]):
        for (kind, t), slot in entries.items():
            row = t - 1 if kind == "hi" else t
            src = jnp.broadcast_to(pairs_ref[h, row:row + 1, :], shape)
            tile = jnp.take_along_axis(src, idx, axis=1)
            o_ref[h, slot] = jnp.where(valid[kind], tile, -jnp.inf)


def _bias_call(rpb, entries):
    depth, heads, n_dr, n_dc = rpb.shape
    assert 2 * n_dc <= 2 * GRID_W
    nxt = jnp.concatenate([rpb[:, :, 1:], jnp.zeros_like(rpb[:, :, :1])], axis=2)
    pairs = jnp.concatenate([rpb, nxt, jnp.zeros((depth, heads, n_dr, 2 * GRID_W - 2 * n_dc), F32)], axis=3)
    nslot = len(entries)
    return pl.pallas_call(
        functools.partial(_bias_body, entries=entries),
        grid=(depth,),
        in_specs=[pl.BlockSpec((None, heads, n_dr, 2 * GRID_W), lambda l: (l, 0, 0, 0))],
        out_specs=pl.BlockSpec((heads, nslot, GRID_W, 2 * GRID_W), lambda l: (l, 0, 0, 0)),
        out_shape=jax.ShapeDtypeStruct((depth * heads, nslot, GRID_W, 2 * GRID_W), F32),
        compiler_params=_params("arbitrary"),
        name="bias",
    )(pairs)


def _na_body(q_ref, k_ref, v_ref, kc_ref, vc_ref, bt_ref, o_ref, *, groups, entries):
    lc = kc_ref.shape[0]
    pair = 2 * GRID_W
    nwin = NA_UNION_ROWS * GRID_W

    def slices(group):
        r0, u0, _ = group
        return (slice(r0 * GRID_W, (r0 + NA_GROUP_ROWS) * GRID_W),
                slice(u0 * GRID_W, (u0 + NA_UNION_ROWS) * GRID_W))

    def scores(head, group):
        qs, us = slices(group)
        hs = slice(head * HEAD_DIM, (head + 1) * HEAD_DIM)
        q = q_ref[qs, hs]
        return _dot_nt(q, k_ref[us, hs]), _dot_nt(q, kc_ref[:, hs])

    def finish(head, group, s_w, s_c):
        qs, us = slices(group)
        hs = slice(head * HEAD_DIM, (head + 1) * HEAD_DIM)
        inv_l, p_rows = [], []
        for i, row in enumerate(group[2]):
            rsl = slice(i * GRID_W, (i + 1) * GRID_W)
            sc_i = s_c[rsl]
            tiles = {p: s_w[rsl, p * pair:(p + 1) * pair] + bt_ref[head, entries[key]]
                     for p, key in enumerate(row) if key is not None}
            m_el = sc_i[:, :pair]
            for c0 in range(pair, lc, pair):
                m_el = jnp.maximum(m_el, sc_i[:, c0:c0 + pair])
            for t in tiles.values():
                m_el = jnp.maximum(m_el, t)
            m = jnp.max(m_el, axis=-1, keepdims=True)
            e_c = jnp.exp(sc_i - m)
            l_el = e_c[:, :pair]
            for c0 in range(pair, lc, pair):
                l_el = l_el + e_c[:, c0:c0 + pair]
            parts = []
            for p in range(len(row)):
                if p in tiles:
                    e = jnp.exp(tiles[p] - m)
                    l_el = l_el + e
                    parts.append(e.astype(BF16))
                else:
                    parts.append(jnp.zeros((GRID_W, pair), BF16))
            parts.append(e_c.astype(BF16))
            p_rows.append(jnp.concatenate(parts, axis=1))
            inv_l.append(1.0 / jnp.sum(l_el, axis=-1, keepdims=True))
        p_all = jnp.concatenate(p_rows, axis=0)
        o = _dot(p_all[:, :nwin], v_ref[us, hs]) + _dot(p_all[:, nwin:], vc_ref[:, hs])
        o_ref[qs, hs] = (o * jnp.concatenate(inv_l, axis=0)).astype(BF16)

    work = [(head, group) for head in range(NA_HEADS_PER_STEP) for group in groups]
    pending = [scores(*w) for w in work[:NA_SCORE_LOOKAHEAD]]
    for wi, w in enumerate(work):
        if wi + NA_SCORE_LOOKAHEAD < len(work):
            pending.append(scores(*work[wi + NA_SCORE_LOOKAHEAD]))
        finish(*w, *pending.pop(0))


def _na_call(hl, hc, ctx_k_block, bt, layer, plan):
    b_, n_, _ = hl.shape
    lc = hc.shape[1]
    groups, entries = plan
    nslot = len(entries)
    hps = NA_HEADS_PER_STEP
    assert N_NA_HEADS % hps == 0 and ctx_k_block % hps == 0
    kofs = N_NA_HEADS // hps
    cofs = ctx_k_block // hps
    width = hps * HEAD_DIM
    body = functools.partial(_na_body, groups=groups, entries=entries)
    return pl.pallas_call(
        body,
        grid=(b_, N_NA_HEADS // hps),
        in_specs=[
            pl.BlockSpec((None, n_, width), lambda b, h: (b, 0, h)),
            pl.BlockSpec((None, n_, width), lambda b, h: (b, 0, kofs + h)),
            pl.BlockSpec((None, n_, width), lambda b, h: (b, 0, 2 * kofs + h)),
            pl.BlockSpec((None, lc, width), lambda b, h: (b, 0, cofs + h)),
            pl.BlockSpec((None, lc, width), lambda b, h: (b, 0, cofs + kofs + h)),
            pl.BlockSpec((hps, nslot, GRID_W, 2 * GRID_W), lambda b, h: (layer * kofs + h, 0, 0, 0)),
        ],
        out_specs=pl.BlockSpec((None, n_, width), lambda b, h: (b, 0, h)),
        out_shape=jax.ShapeDtypeStruct((b_, n_, NA_WIDTH), BF16),
        compiler_params=_params("arbitrary", "arbitrary"),
        name="na",
    )(hl, hl, hl, hc, hc, bt)


def _ctx_attn_body(q_ref, k_ref, v_ref, o_ref):
    for h in range(N_NA_HEADS):
        sl = slice(h * HEAD_DIM, (h + 1) * HEAD_DIM)
        s = _dot_nt(q_ref[:, sl], k_ref[:, sl])
        m = jnp.max(s, axis=-1, keepdims=True)
        e = jnp.exp(s - m)
        l = jnp.sum(e, axis=-1, keepdims=True)
        o_ref[:, sl] = (_dot(e.astype(BF16), v_ref[:, sl]) / l).astype(BF16)


def _ctx_attn_call(hc):
    b_, lc, _ = hc.shape
    return pl.pallas_call(
        _ctx_attn_body,
        grid=(b_,),
        in_specs=[
            pl.BlockSpec((None, lc, NA_WIDTH), lambda b: (b, 0, 0)),
            pl.BlockSpec((None, lc, NA_WIDTH), lambda b: (b, 0, 1)),
            pl.BlockSpec((None, lc, NA_WIDTH), lambda b: (b, 0, 2)),
        ],
        out_specs=pl.BlockSpec((None, lc, NA_WIDTH), lambda b: (b, 0, 0)),
        out_shape=jax.ShapeDtypeStruct((b_, lc, NA_WIDTH), BF16),
        compiler_params=_params("arbitrary"),
        name="ctx_attn",
    )(hc, hc, hc)


def _dft_tables(length):
    def cos_sin(freq, n_pos, period):
        ang = ((freq[:, None] * jnp.arange(n_pos, dtype=jnp.int32)[None, :]) % period).astype(F32)
        ang = ang * (2.0 * math.pi / period)
        return jnp.cos(ang), jnp.sin(ang)

    assert length % DFT_ROW_BLOCK == 0
    coarse = jnp.arange(length // DFT_ROW_BLOCK, dtype=jnp.int32) * DFT_ROW_BLOCK
    fine = jnp.arange(DFT_ROW_BLOCK, dtype=jnp.int32)
    ca, sa = cos_sin(coarse, length, length)
    cb, sb = cos_sin(fine, length, length)
    cc, sc = cos_sin(jnp.arange(FOUR_DIM, dtype=jnp.int32), FOUR_DIM, FOUR_DIM)
    scale = 1.0 / math.sqrt(length * FOUR_DIM)
    chan = jnp.concatenate([cc * scale, sc * scale], axis=1).astype(BF16)
    return ca, sa, cb, sb, chan


def _fourier_body(f_ref, chan_ref, ca_ref, sa_ref, cb_ref, sb_ref, wf_ref, o_ref, cl_scr, sl_scr, zc_scr, zs_scr):
    @pl.when(pl.program_id(0) == 0)
    def _():
        cb = cb_ref[...]
        sb = sb_ref[...]
        for a in range(ca_ref.shape[0]):
            rows = slice(a * DFT_ROW_BLOCK, (a + 1) * DFT_ROW_BLOCK)
            ca = ca_ref[a:a + 1, :]
            sa = sa_ref[a:a + 1, :]
            cl_scr[rows, :] = (ca * cb - sa * sb).astype(BF16)
            sl_scr[rows, :] = (-(sa * cb + ca * sb)).astype(BF16)

    for g in range(N_FOUR_GROUPS):
        sl = slice(g * FOUR_DIM, (g + 1) * FOUR_DIM)
        z = _dot(f_ref[:, sl], chan_ref[...])
        zc_scr[:, sl] = z[:, :FOUR_DIM].astype(BF16)
        zs_scr[:, sl] = z[:, FOUR_DIM:].astype(BF16)
    mixed = _dot(cl_scr[...], zc_scr[...]) + _dot(sl_scr[...], zs_scr[...])
    for g in range(N_FOUR_GROUPS):
        sl = slice(g * FOUR_DIM, (g + 1) * FOUR_DIM)
        o_ref[:, sl] = _dot(mixed[:, sl].astype(BF16), wf_ref[g].astype(BF16)).astype(BF16)


def _fourier_call(h, tables, w_four, layer):
    b_, length, _ = h.shape
    ca, sa, cb, sb, chan = tables
    fblk = 3 * NA_WIDTH // FOUR_WIDTH
    whole = lambda arr: pl.BlockSpec(arr.shape, lambda b: (0,) * arr.ndim)
    return pl.pallas_call(
        _fourier_body,
        grid=(b_,),
        in_specs=[
            pl.BlockSpec((None, length, FOUR_WIDTH), lambda b: (b, 0, fblk)),
            whole(chan), whole(ca), whole(sa), whole(cb), whole(sb),
            pl.BlockSpec((None, N_FOUR_GROUPS, FOUR_DIM, FOUR_DIM), lambda b: (layer, 0, 0, 0)),
        ],
        out_specs=pl.BlockSpec((None, length, FOUR_WIDTH), lambda b: (b, 0, 0)),
        out_shape=jax.ShapeDtypeStruct((b_, length, FOUR_WIDTH), BF16),
        scratch_shapes=[pltpu.VMEM((length, length), BF16), pltpu.VMEM((length, length), BF16),
                        pltpu.VMEM((length, FOUR_WIDTH), BF16), pltpu.VMEM((length, FOUR_WIDTH), BF16)],
        compiler_params=_params("arbitrary"),
        name="fourier",
    )(h, chan, ca, sa, cb, sb, w_four)


def _proj_out_body(x_ref, g_ref, na_ref, fo_ref, w_ref, o_ref, wb_scr):
    @pl.when((pl.program_id(0) == 0) & (pl.program_id(1) == 0))
    def _():
        wb_scr[...] = w_ref[...].astype(BF16)

    y = _dot(na_ref[...], wb_scr[:NA_WIDTH, :]) + _dot(fo_ref[...], wb_scr[NA_WIDTH:, :])
    o_ref[...] = x_ref[...] + g_ref[...] * y


def _proj_out_call(x, gate, na, fo, w_out, layer, tm=512):
    g_, r_, d = x.shape
    kin = w_out.shape[1]
    return pl.pallas_call(
        _proj_out_body,
        grid=(g_, r_ // tm),
        in_specs=[
            pl.BlockSpec((None, tm, d), lambda g, i: (g, i, 0)),
            pl.BlockSpec((None, 1, d), lambda g, i: (g, 0, 0)),
            pl.BlockSpec((None, tm, NA_WIDTH), lambda g, i: (g, i, 0)),
            pl.BlockSpec((None, tm, FOUR_WIDTH), lambda g, i: (g, i, 0)),
            pl.BlockSpec((None, kin, d), lambda g, i: (layer, 0, 0), pipeline_mode=pl.Buffered(1)),
        ],
        out_specs=pl.BlockSpec((None, tm, d), lambda g, i: (g, i, 0)),
        out_shape=jax.ShapeDtypeStruct(x.shape, F32),
        scratch_shapes=[pltpu.VMEM((kin, d), BF16)],
        compiler_params=_params("arbitrary", "arbitrary"),
        name="proj_out",
    )(x, gate, na, fo, w_out)


def kernel(x, c, ctx, c_ctx, w_mod, b_mod, norm_w, ffn1_wi, ffn1_wo, w_in, q_norm_w, k_norm_w, rpb,
           w_four, w_out, ffn2_wi, ffn2_wo):
    b_, n_, d = x.shape
    lc = ctx.shape[1]
    depth = w_mod.shape[0]
    rows = n_ // GRID_W

    cvec = jnp.zeros((MOD_ROWS, d), F32).at[:b_].set(c).at[b_].set(c_ctx)
    mod = _mod_call(cvec, w_mod, b_mod).reshape(depth, MOD_ROWS, N_MOD, d)

    plan = _na_plan(rows, min(WIN_ROWS, rows))
    bt = _bias_call(rpb, plan[1])
    tables_lat = _dft_tables(n_)
    tables_ctx = _dft_tables(lc)
    ones = jnp.ones((NA_WIDTH + FOUR_WIDTH,), F32)

    xc = ctx.reshape(1, b_ * lc, d)
    for l in range(depth):
        last = l == depth - 1
        m = [mod[l, :b_, i].reshape(b_, 1, d) for i in range(N_MOD)]
        mc = [mod[l, b_, i].reshape(1, 1, d) for i in range(N_MOD)]
        nw = [norm_w[l, i].reshape(1, d) for i in range(3)]
        gain = jnp.concatenate([
            jnp.tile(q_norm_w[l] * HEAD_DIM ** -0.5, N_NA_HEADS),
            jnp.tile(k_norm_w[l], N_NA_HEADS), ones]).reshape(1, IN_WIDTH)

        x = _ffn_call(x, m[0], m[1], m[2], nw[0], ffn1_wi, ffn1_wo, l)
        xc = _ffn_call(xc, mc[0], mc[1], mc[2], nw[0], ffn1_wi, ffn1_wo, l)

        hl = _proj_in_call(x, m[3], m[4], nw[1], w_in, gain, l)
        ctx_cols = (NA_WIDTH, 3 * NA_WIDTH) if last else (0, IN_WIDTH)
        hc = _proj_in_call(xc, mc[3], mc[4], nw[1], w_in, gain, l, col_range=ctx_cols)
        hc = hc.reshape(b_, lc, ctx_cols[1] - ctx_cols[0])

        na = _na_call(hl, hc, (NA_WIDTH - ctx_cols[0]) // HEAD_DIM, bt, l, plan)
        fo = _fourier_call(hl, tables_lat, w_four, l)
        x = _proj_out_call(x, m[5], na, fo, w_out, l)

        if not last:
            nac = _ctx_attn_call(hc).reshape(1, b_ * lc, NA_WIDTH)
            foc = _fourier_call(hc, tables_ctx, w_four, l).reshape(1, b_ * lc, FOUR_WIDTH)
            xc = _proj_out_call(xc, mc[5], nac, foc, w_out, l)
            xc = _ffn_call(xc, mc[6], mc[7], mc[8], nw[2], ffn2_wi, ffn2_wo, l)

        x = _ffn_call(x, m[6], m[7], m[8], nw[2], ffn2_wi, ffn2_wo, l)
    return x
```

```python
import functools
import math

import jax
import jax.numpy as jnp
from jax import lax
from jax.experimental import pallas as pl
from jax.experimental.pallas import tpu as pltpu

GRID_W = 64
HEAD_DIM = 128
N_NA_HEADS = 12
NA_WIDTH = N_NA_HEADS * HEAD_DIM
N_FOUR_GROUPS = 4
FOUR_DIM = 128
FOUR_WIDTH = N_FOUR_GROUPS * FOUR_DIM
IN_WIDTH = 3 * NA_WIDTH + FOUR_WIDTH
WIN_ROWS = 8
WIN_COLS = 16
N_MOD = 9
EPS = 1e-6

BF16 = jnp.bfloat16
F32 = jnp.float32

V7X_VMEM_LIMIT_BYTES = 56 * 1024 * 1024
MOD_ROWS = 8
RMS_CHUNK_ROWS = 64
LANES = 128
DFT_ROW_BLOCK = 64
NA_GROUP_ROWS = 4
NA_HEADS_PER_STEP = 4
NA_SCORE_LOOKAHEAD = 2
NA_UNION_ROWS = 12


def _params(*sem):
    return pltpu.CompilerParams(dimension_semantics=sem, vmem_limit_bytes=V7X_VMEM_LIMIT_BYTES)


def _dot(a, b):
    return jnp.dot(a, b, preferred_element_type=F32)


def _dot_nt(a, b):
    return lax.dot_general(a, b, (((1,), (1,)), ((), ())), preferred_element_type=F32)


def _silu(x):
    return x / (1.0 + jnp.exp(-x))


def _rms_modulate_rows(x_ref, nw_ref, sh_ref, sc_ref, h_ref, rs_ref, copy_ref=None):
    n_chunks = x_ref.shape[0] // RMS_CHUNK_ROWS

    def rows_of(c):
        return pl.ds(pl.multiple_of(c * RMS_CHUNK_ROWS, RMS_CHUNK_ROWS), RMS_CHUNK_ROWS)

    def row_scale(c, carry):
        x = x_ref[rows_of(c), :]
        rs = lax.rsqrt(jnp.mean(x * x, axis=-1, keepdims=True) + EPS)
        rs_ref[rows_of(c), :] = jnp.broadcast_to(rs, (RMS_CHUNK_ROWS, rs_ref.shape[1]))
        return carry

    lax.fori_loop(0, n_chunks, row_scale, 0, unroll=4)

    def scale_rows(c, carry):
        rs = rs_ref[rows_of(c), :]
        for c0 in range(0, x_ref.shape[1], LANES):
            cols = slice(c0, c0 + LANES)
            x = x_ref[rows_of(c), cols]
            y = ((x * rs) * nw_ref[:, cols]) * (1.0 + sc_ref[:, cols]) + sh_ref[:, cols]
            h_ref[rows_of(c), cols] = y.astype(BF16)
            if copy_ref is not None:
                copy_ref[rows_of(c), cols] = x
        return carry

    lax.fori_loop(0, n_chunks, scale_rows, 0, unroll=2)


def _mod_body(c_ref, w_ref, b_ref, o_ref):
    s = _silu(c_ref[...]).astype(BF16)
    o_ref[...] = _dot(s, w_ref[...].astype(BF16)) + b_ref[...]


def _mod_call(cvec, w_mod, b_mod, tn=2048):
    depth, d, nout = w_mod.shape
    return pl.pallas_call(
        _mod_body,
        grid=(depth, nout // tn),
        in_specs=[
            pl.BlockSpec((MOD_ROWS, d), lambda l, j: (0, 0)),
            pl.BlockSpec((None, d, tn), lambda l, j: (l, 0, j)),
            pl.BlockSpec((None, 1, tn), lambda l, j: (l, 0, j)),
        ],
        out_specs=pl.BlockSpec((None, MOD_ROWS, tn), lambda l, j: (l, 0, j)),
        out_shape=jax.ShapeDtypeStruct((depth, MOD_ROWS, nout), F32),
        compiler_params=_params("arbitrary", "arbitrary"),
        name="mod",
    )(cvec, w_mod, b_mod.reshape(depth, 1, nout))


def _ffn_body(x_ref, sh_ref, sc_ref, g_ref, nw_ref, wi_hbm, wo_hbm, o_ref,
              h_scr, rs_scr, wg_buf, wu_buf, wo_buf, sem, *, layer, tf, nf, grid):
    tile = pl.program_id(0) * grid[1] + pl.program_id(1)
    n_tiles = grid[0] * grid[1]
    ff = nf * tf

    def weight_copies(f, slot):
        c0 = pl.multiple_of(f * tf, tf)
        c1 = pl.multiple_of(ff + f * tf, tf)
        return (
            pltpu.make_async_copy(wi_hbm.at[layer, :, pl.ds(c0, tf)], wg_buf.at[slot], sem.at[0, slot]),
            pltpu.make_async_copy(wi_hbm.at[layer, :, pl.ds(c1, tf)], wu_buf.at[slot], sem.at[1, slot]),
            pltpu.make_async_copy(wo_hbm.at[layer, pl.ds(c0, tf), :], wo_buf.at[slot], sem.at[2, slot]),
        )

    @pl.when(tile == 0)
    def _():
        for cp in weight_copies(0, 0):
            cp.start()

    _rms_modulate_rows(x_ref, nw_ref, sh_ref, sc_ref, h_scr, rs_scr, copy_ref=o_ref)
    gate = 0.5 * g_ref[...]

    def sub_step(f, slot, f_next, has_next):
        for cp in weight_copies(f, slot):
            cp.wait()

        @pl.when(has_next)
        def _():
            for cp in weight_copies(f_next, 1 - slot):
                cp.start()

        h = h_scr[...]
        g = _dot(h, wg_buf[slot].astype(BF16))
        u = _dot(h, wu_buf[slot].astype(BF16))
        a = (_silu(g) * u).astype(BF16)
        o_ref[...] += gate * _dot(a, wo_buf[slot].astype(BF16))

    def pair(p, carry):
        f0 = 2 * p
        sub_step(f0, 0, f0 + 1, True)
        sub_step(f0 + 1, 1, f0 + 2, True)
        return carry

    lax.fori_loop(0, nf // 2 - 1, pair, 0)
    sub_step(nf - 2, 0, nf - 1, True)
    sub_step(nf - 1, 1, 0, tile + 1 < n_tiles)


def _ffn_call(x, shift, scale, gate, nw, wi, wo, layer, tm=1024, tf=256):
    g_, r_, d = x.shape
    ff = wo.shape[1]
    nf = ff // tf
    assert nf % 2 == 0 and nf * tf == ff
    vec = pl.BlockSpec((None, 1, d), lambda g, i: (g, 0, 0))
    grid = (g_, r_ // tm)
    return pl.pallas_call(
        functools.partial(_ffn_body, layer=layer, tf=tf, nf=nf, grid=grid),
        grid=grid,
        in_specs=[
            pl.BlockSpec((None, tm, d), lambda g, i: (g, i, 0)),
            vec, vec, vec,
            pl.BlockSpec((1, d), lambda g, i: (0, 0)),
            pl.BlockSpec(memory_space=pl.ANY),
            pl.BlockSpec(memory_space=pl.ANY),
        ],
        out_specs=pl.BlockSpec((None, tm, d), lambda g, i: (g, i, 0)),
        out_shape=jax.ShapeDtypeStruct(x.shape, F32),
        scratch_shapes=[
            pltpu.VMEM((tm, d), BF16), pltpu.VMEM((tm, LANES), F32),
            pltpu.VMEM((2, d, tf), F32), pltpu.VMEM((2, d, tf), F32), pltpu.VMEM((2, tf, d), F32),
            pltpu.SemaphoreType.DMA((3, 2)),
        ],
        compiler_params=_params("arbitrary", "arbitrary"),
        name="ffn",
    )(x, shift, scale, gate, nw, wi, wo)


def _proj_in_body(x_ref, sh_ref, sc_ref, nw_ref, w_hbm, gain_ref, o_ref, h_scr, rs_scr, w_buf, sem,
                  *, layer, tn, n_norm_tiles, col_tiles, grid):
    tile = pl.program_id(0) * grid[1] + pl.program_id(1)
    n_tiles = grid[0] * grid[1]
    n_lo, n_hi = col_tiles

    def w_copy(n):
        slot = (n - n_lo) % 2
        return pltpu.make_async_copy(w_hbm.at[layer, :, pl.ds(n * tn, tn)], w_buf.at[slot], sem.at[slot])

    @pl.when(tile == 0)
    def _():
        w_copy(n_lo).start()
        w_copy(n_lo + 1).start()

    _rms_modulate_rows(x_ref, nw_ref, sh_ref, sc_ref, h_scr, rs_scr)

    w_copy(n_lo).wait()
    for n in range(n_lo, n_hi):
        slot = (n - n_lo) % 2
        acc = _dot(h_scr[...], w_buf[slot].astype(BF16))

        if n + 1 < n_hi:
            w_copy(n + 1).wait()
        if n + 2 < n_hi:
            w_copy(n + 2).start()
        else:
            @pl.when(tile + 1 < n_tiles)
            def _():
                w_copy(n + 2 - n_hi + n_lo).start()

        if n < n_norm_tiles:
            for hh in range(tn // HEAD_DIM):
                sl = slice(hh * HEAD_DIM, (hh + 1) * HEAD_DIM)
                gsl = slice(n * tn + hh * HEAD_DIM, n * tn + (hh + 1) * HEAD_DIM)
                osl = slice((n - n_lo) * tn + hh * HEAD_DIM, (n - n_lo) * tn + (hh + 1) * HEAD_DIM)
                t = acc[:, sl]
                y = t * lax.rsqrt(jnp.mean(t * t, axis=-1, keepdims=True) + EPS)
                o_ref[:, osl] = (y * gain_ref[:, gsl]).astype(BF16)
        else:
            o_ref[:, (n - n_lo) * tn:(n - n_lo + 1) * tn] = acc.astype(BF16)


def _proj_in_call(x, shift, scale, nw, w_in, gain, layer, col_range=None, tm=1024, tn=512):
    g_, r_, d = x.shape
    win = w_in.shape[2]
    lo, hi = (0, win) if col_range is None else col_range
    assert lo % tn == 0 and hi % tn == 0 and ((hi - lo) // tn) % 2 == 0 and (2 * NA_WIDTH) % tn == 0
    nout = hi - lo
    grid = (g_, r_ // tm)
    vec = pl.BlockSpec((None, 1, d), lambda g, i: (g, 0, 0))
    body = functools.partial(_proj_in_body, layer=layer, tn=tn, n_norm_tiles=2 * NA_WIDTH // tn,
                             col_tiles=(lo // tn, hi // tn), grid=grid)
    return pl.pallas_call(
        body,
        grid=grid,
        in_specs=[
            pl.BlockSpec((None, tm, d), lambda g, i: (g, i, 0)),
            vec, vec,
            pl.BlockSpec((1, d), lambda g, i: (0, 0)),
            pl.BlockSpec(memory_space=pl.ANY),
            pl.BlockSpec((1, win), lambda g, i: (0, 0)),
        ],
        out_specs=pl.BlockSpec((None, tm, nout), lambda g, i: (g, i, 0)),
        out_shape=jax.ShapeDtypeStruct((g_, r_, nout), BF16),
        scratch_shapes=[pltpu.VMEM((tm, d), BF16), pltpu.VMEM((tm, LANES), F32),
                        pltpu.VMEM((2, d, tn), F32), pltpu.SemaphoreType.DMA((2,))],
        compiler_params=_params("arbitrary", "arbitrary"),
        name="proj_in",
    )(x, shift, scale, nw, w_in, gain)


def _na_plan(rows, kh):
    assert rows % NA_GROUP_ROWS == 0 and rows >= NA_UNION_ROWS and NA_UNION_ROWS % 2 == 0
    assert NA_UNION_ROWS >= NA_GROUP_ROWS + kh - 1
    groups, entries = [], {}
    for r0 in range(0, rows, NA_GROUP_ROWS):
        u0 = min(max(r0 - kh // 2, 0), rows - NA_UNION_ROWS)
        keys = []
        for r in range(r0, r0 + NA_GROUP_ROWS):
            rs = min(max(r - kh // 2, 0), rows - kh)
            assert u0 <= rs and rs + kh <= u0 + NA_UNION_ROWS
            row = []
            for j0 in range(u0, u0 + NA_UNION_ROWS, 2):
                v0, v1 = rs <= j0 < rs + kh, rs <= j0 + 1 < rs + kh
                if v0 and v1:
                    key = ("both", j0 - r + WIN_ROWS - 1)
                elif v0:
                    key = ("lo", j0 - r + WIN_ROWS - 1)
                elif v1:
                    key = ("hi", j0 + 1 - r + WIN_ROWS - 1)
                else:
                    key = None
                if key is not None:
                    entries.setdefault(key, len(entries))
                row.append(key)
            keys.append(row)
        groups.append((r0, u0, keys))
    return groups, entries


def _bias_body(pairs_ref, o_ref, *, entries):
    shape = (GRID_W, 2 * GRID_W)
    n_dc = 2 * WIN_COLS - 1
    cq = lax.broadcasted_iota(jnp.int32, shape, 0)
    lane = lax.broadcasted_iota(jnp.int32, shape, 1)
    ck = lane & (GRID_W - 1)
    hi = lane >= GRID_W
    dc = jnp.clip(ck - cq, -(WIN_COLS - 1), WIN_COLS - 1) + (WIN_COLS - 1)
    idx = dc + jnp.where(hi, n_dc, 0)
    col_start = jnp.clip(cq - WIN_COLS // 2, 0, GRID_W - WIN_COLS)
    col_in = (ck >= col_start) & (ck < col_start + WIN_COLS)
    valid = {"both": col_in, "hi": col_in & hi, "lo": col_in & ~hi}
    for h in range(o_ref.shape[0]):
        for (kind, t), slot in entries.items():
            row = t - 1 if kind == "hi" else t
            src = jnp.broadcast_to(pairs_ref[h, row:row + 1, :], shape)
            tile = jnp.take_along_axis(src, idx, axis=1)
            o_ref[h, slot] = jnp.where(valid[kind], tile, -jnp.inf)


def _bias_call(rpb, entries):
    depth, heads, n_dr, n_dc = rpb.shape
    assert 2 * n_dc <= 2 * GRID_W
    nxt = jnp.concatenate([rpb[:, :, 1:], jnp.zeros_like(rpb[:, :, :1])], axis=2)
    pairs = jnp.concatenate([rpb, nxt, jnp.zeros((depth, heads, n_dr, 2 * GRID_W - 2 * n_dc), F32)], axis=3)
    nslot = len(entries)
    return pl.pallas_call(
        functools.partial(_bias_body, entries=entries),
        grid=(depth,),
        in_specs=[pl.BlockSpec((None, heads, n_dr, 2 * GRID_W), lambda l: (l, 0, 0, 0))],
        out_specs=pl.BlockSpec((heads, nslot, GRID_W, 2 * GRID_W), lambda l: (l, 0, 0, 0)),
        out_shape=jax.ShapeDtypeStruct((depth * heads, nslot, GRID_W, 2 * GRID_W), F32),
        compiler_params=_params("arbitrary"),
        name="bias",
    )(pairs)


def _na_body(q_ref, k_ref, v_ref, kc_ref, vc_ref, bt_ref, o_ref, *, groups, entries):
    lc = kc_ref.shape[0]
    pair = 2 * GRID_W
    nwin = NA_UNION_ROWS * GRID_W

    def slices(group):
        r0, u0, _ = group
        return (slice(r0 * GRID_W, (r0 + NA_GROUP_ROWS) * GRID_W),
                slice(u0 * GRID_W, (u0 + NA_UNION_ROWS) * GRID_W))

    def scores(head, group):
        qs, us = slices(group)
        hs = slice(head * HEAD_DIM, (head + 1) * HEAD_DIM)
        q = q_ref[qs, hs]
        return _dot_nt(q, k_ref[us, hs]), _dot_nt(q, kc_ref[:, hs])

    def finish(head, group, s_w, s_c):
        qs, us = slices(group)
        hs = slice(head * HEAD_DIM, (head + 1) * HEAD_DIM)
        inv_l, p_rows = [], []
        for i, row in enumerate(group[2]):
            rsl = slice(i * GRID_W, (i + 1) * GRID_W)
            sc_i = s_c[rsl]
            tiles = {p: s_w[rsl, p * pair:(p + 1) * pair] + bt_ref[head, entries[key]]
                     for p, key in enumerate(row) if key is not None}
            m_el = sc_i[:, :pair]
            for c0 in range(pair, lc, pair):
                m_el = jnp.maximum(m_el, sc_i[:, c0:c0 + pair])
            for t in tiles.values():
                m_el = jnp.maximum(m_el, t)
            m = jnp.max(m_el, axis=-1, keepdims=True)
            e_c = jnp.exp(sc_i - m)
            l_el = e_c[:, :pair]
            for c0 in range(pair, lc, pair):
                l_el = l_el + e_c[:, c0:c0 + pair]
            parts = []
            for p in range(len(row)):
                if p in tiles:
                    e = jnp.exp(tiles[p] - m)
                    l_el = l_el + e
                    parts.append(e.astype(BF16))
                else:
                    parts.append(jnp.zeros((GRID_W, pair), BF16))
            parts.append(e_c.astype(BF16))
            p_rows.append(jnp.concatenate(parts, axis=1))
            inv_l.append(1.0 / jnp.sum(l_el, axis=-1, keepdims=True))
        return jnp.concatenate(p_rows, axis=0), jnp.concatenate(inv_l, axis=0)

    def weighted_values(head, group, p_all, inv_l):
        qs, us = slices(group)
        hs = slice(head * HEAD_DIM, (head + 1) * HEAD_DIM)
        o = _dot(p_all[:, :nwin], v_ref[us, hs]) + _dot(p_all[:, nwin:], vc_ref[:, hs])
        o_ref[qs, hs] = (o * inv_l).astype(BF16)

    work = [(head, group) for head in range(NA_HEADS_PER_STEP) for group in groups]
    pending = [scores(*w) for w in work[:NA_SCORE_LOOKAHEAD]]
    probs = None
    for wi, w in enumerate(work):
        if wi + NA_SCORE_LOOKAHEAD < len(work):
            pending.append(scores(*work[wi + NA_SCORE_LOOKAHEAD]))
        new_probs = (w, finish(*w, *pending.pop(0)))
        if probs is not None:
            weighted_values(*probs[0], *probs[1])
        probs = new_probs
    weighted_values(*probs[0], *probs[1])


def _na_call(hl, hc, ctx_k_block, bt, layer, plan):
    b_, n_, _ = hl.shape
    lc = hc.shape[1]
    groups, entries = plan
    nslot = len(entries)
    hps = NA_HEADS_PER_STEP
    assert N_NA_HEADS % hps == 0 and ctx_k_block % hps == 0
    kofs = N_NA_HEADS // hps
    cofs = ctx_k_block // hps
    width = hps * HEAD_DIM
    body = functools.partial(_na_body, groups=groups, entries=entries)
    return pl.pallas_call(
        body,
        grid=(b_, N_NA_HEADS // hps),
        in_specs=[
            pl.BlockSpec((None, n_, width), lambda b, h: (b, 0, h)),
            pl.BlockSpec((None, n_, width), lambda b, h: (b, 0, kofs + h)),
            pl.BlockSpec((None, n_, width), lambda b, h: (b, 0, 2 * kofs + h)),
            pl.BlockSpec((None, lc, width), lambda b, h: (b, 0, cofs + h)),
            pl.BlockSpec((None, lc, width), lambda b, h: (b, 0, cofs + kofs + h)),
            pl.BlockSpec((hps, nslot, GRID_W, 2 * GRID_W), lambda b, h: (layer * kofs + h, 0, 0, 0)),
        ],
        out_specs=pl.BlockSpec((None, n_, width), lambda b, h: (b, 0, h)),
        out_shape=jax.ShapeDtypeStruct((b_, n_, NA_WIDTH), BF16),
        compiler_params=_params("arbitrary", "arbitrary"),
        name="na",
    )(hl, hl, hl, hc, hc, bt)


def _ctx_attn_body(q_ref, k_ref, v_ref, o_ref):
    for h in range(N_NA_HEADS):
        sl = slice(h * HEAD_DIM, (h + 1) * HEAD_DIM)
        s = _dot_nt(q_ref[:, sl], k_ref[:, sl])
        m = jnp.max(s, axis=-1, keepdims=True)
        e = jnp.exp(s - m)
        l = jnp.sum(e, axis=-1, keepdims=True)
        o_ref[:, sl] = (_dot(e.astype(BF16), v_ref[:, sl]) / l).astype(BF16)


def _ctx_attn_call(hc):
    b_, lc, _ = hc.shape
    return pl.pallas_call(
        _ctx_attn_body,
        grid=(b_,),
        in_specs=[
            pl.BlockSpec((None, lc, NA_WIDTH), lambda b: (b, 0, 0)),
            pl.BlockSpec((None, lc, NA_WIDTH), lambda b: (b, 0, 1)),
            pl.BlockSpec((None, lc, NA_WIDTH), lambda b: (b, 0, 2)),
        ],
        out_specs=pl.BlockSpec((None, lc, NA_WIDTH), lambda b: (b, 0, 0)),
        out_shape=jax.ShapeDtypeStruct((b_, lc, NA_WIDTH), BF16),
        compiler_params=_params("arbitrary"),
        name="ctx_attn",
    )(hc, hc, hc)


def _dft_tables(length):
    def cos_sin(freq, n_pos, period):
        ang = ((freq[:, None] * jnp.arange(n_pos, dtype=jnp.int32)[None, :]) % period).astype(F32)
        ang = ang * (2.0 * math.pi / period)
        return jnp.cos(ang), jnp.sin(ang)

    assert length % DFT_ROW_BLOCK == 0
    coarse = jnp.arange(length // DFT_ROW_BLOCK, dtype=jnp.int32) * DFT_ROW_BLOCK
    fine = jnp.arange(DFT_ROW_BLOCK, dtype=jnp.int32)
    ca, sa = cos_sin(coarse, length, length)
    cb, sb = cos_sin(fine, length, length)
    cc, sc = cos_sin(jnp.arange(FOUR_DIM, dtype=jnp.int32), FOUR_DIM, FOUR_DIM)
    scale = 1.0 / math.sqrt(length * FOUR_DIM)
    chan = jnp.concatenate([cc * scale, sc * scale], axis=1).astype(BF16)
    return ca, sa, cb, sb, chan


def _fourier_body(f_ref, chan_ref, ca_ref, sa_ref, cb_ref, sb_ref, wf_ref, o_ref, cl_scr, sl_scr, zc_scr, zs_scr):
    @pl.when(pl.program_id(0) == 0)
    def _():
        cb = cb_ref[...]
        sb = sb_ref[...]
        for a in range(ca_ref.shape[0]):
            rows = slice(a * DFT_ROW_BLOCK, (a + 1) * DFT_ROW_BLOCK)
            ca = ca_ref[a:a + 1, :]
            sa = sa_ref[a:a + 1, :]
            cl_scr[rows, :] = (ca * cb - sa * sb).astype(BF16)
            sl_scr[rows, :] = (-(sa * cb + ca * sb)).astype(BF16)

    for g in range(N_FOUR_GROUPS):
        sl = slice(g * FOUR_DIM, (g + 1) * FOUR_DIM)
        z = _dot(f_ref[:, sl], chan_ref[...])
        zc_scr[:, sl] = z[:, :FOUR_DIM].astype(BF16)
        zs_scr[:, sl] = z[:, FOUR_DIM:].astype(BF16)
    mixed = _dot(cl_scr[...], zc_scr[...]) + _dot(sl_scr[...], zs_scr[...])
    for g in range(N_FOUR_GROUPS):
        sl = slice(g * FOUR_DIM, (g + 1) * FOUR_DIM)
        o_ref[:, sl] = _dot(mixed[:, sl].astype(BF16), wf_ref[g].astype(BF16)).astype(BF16)


def _fourier_call(h, tables, w_four, layer):
    b_, length, _ = h.shape
    ca, sa, cb, sb, chan = tables
    fblk = 3 * NA_WIDTH // FOUR_WIDTH
    whole = lambda arr: pl.BlockSpec(arr.shape, lambda b: (0,) * arr.ndim)
    return pl.pallas_call(
        _fourier_body,
        grid=(b_,),
        in_specs=[
            pl.BlockSpec((None, length, FOUR_WIDTH), lambda b: (b, 0, fblk)),
            whole(chan), whole(ca), whole(sa), whole(cb), whole(sb),
            pl.BlockSpec((None, N_FOUR_GROUPS, FOUR_DIM, FOUR_DIM), lambda b: (layer, 0, 0, 0)),
        ],
        out_specs=pl.BlockSpec((None, length, FOUR_WIDTH), lambda b: (b, 0, 0)),
        out_shape=jax.ShapeDtypeStruct((b_, length, FOUR_WIDTH), BF16),
        scratch_shapes=[pltpu.VMEM((length, length), BF16), pltpu.VMEM((length, length), BF16),
                        pltpu.VMEM((length, FOUR_WIDTH), BF16), pltpu.VMEM((length, FOUR_WIDTH), BF16)],
        compiler_params=_params("arbitrary"),
        name="fourier",
    )(h, chan, ca, sa, cb, sb, w_four)


def _proj_out_body(x_ref, g_ref, na_ref, fo_ref, w_ref, o_ref, wb_scr):
    @pl.when((pl.program_id(0) == 0) & (pl.program_id(1) == 0))
    def _():
        wb_scr[...] = w_ref[...].astype(BF16)

    y = _dot(na_ref[...], wb_scr[:NA_WIDTH, :]) + _dot(fo_ref[...], wb_scr[NA_WIDTH:, :])
    o_ref[...] = x_ref[...] + g_ref[...] * y


def _proj_out_call(x, gate, na, fo, w_out, layer, tm=512):
    g_, r_, d = x.shape
    kin = w_out.shape[1]
    return pl.pallas_call(
        _proj_out_body,
        grid=(g_, r_ // tm),
        in_specs=[
            pl.BlockSpec((None, tm, d), lambda g, i: (g, i, 0)),
            pl.BlockSpec((None, 1, d), lambda g, i: (g, 0, 0)),
            pl.BlockSpec((None, tm, NA_WIDTH), lambda g, i: (g, i, 0)),
            pl.BlockSpec((None, tm, FOUR_WIDTH), lambda g, i: (g, i, 0)),
            pl.BlockSpec((None, kin, d), lambda g, i: (layer, 0, 0), pipeline_mode=pl.Buffered(1)),
        ],
        out_specs=pl.BlockSpec((None, tm, d), lambda g, i: (g, i, 0)),
        out_shape=jax.ShapeDtypeStruct(x.shape, F32),
        scratch_shapes=[pltpu.VMEM((kin, d), BF16)],
        compiler_params=_params("arbitrary", "arbitrary"),
        name="proj_out",
    )(x, gate, na, fo, w_out)


def kernel(x, c, ctx, c_ctx, w_mod, b_mod, norm_w, ffn1_wi, ffn1_wo, w_in, q_norm_w, k_norm_w, rpb,
           w_four, w_out, ffn2_wi, ffn2_wo):
    b_, n_, d = x.shape
    lc = ctx.shape[1]
    depth = w_mod.shape[0]
    rows = n_ // GRID_W

    cvec = jnp.zeros((MOD_ROWS, d), F32).at[:b_].set(c).at[b_].set(c_ctx)
    mod = _mod_call(cvec, w_mod, b_mod).reshape(depth, MOD_ROWS, N_MOD, d)

    plan = _na_plan(rows, min(WIN_ROWS, rows))
    bt = _bias_call(rpb, plan[1])
    tables_lat = _dft_tables(n_)
    tables_ctx = _dft_tables(lc)
    ones = jnp.ones((NA_WIDTH + FOUR_WIDTH,), F32)

    xc = ctx.reshape(1, b_ * lc, d)
    for l in range(depth):
        last = l == depth - 1
        m = [mod[l, :b_, i].reshape(b_, 1, d) for i in range(N_MOD)]
        mc = [mod[l, b_, i].reshape(1, 1, d) for i in range(N_MOD)]
        nw = [norm_w[l, i].reshape(1, d) for i in range(3)]
        gain = jnp.concatenate([
            jnp.tile(q_norm_w[l] * HEAD_DIM ** -0.5, N_NA_HEADS),
            jnp.tile(k_norm_w[l], N_NA_HEADS), ones]).reshape(1, IN_WIDTH)

        x = _ffn_call(x, m[0], m[1], m[2], nw[0], ffn1_wi, ffn1_wo, l)
        xc = _ffn_call(xc, mc[0], mc[1], mc[2], nw[0], ffn1_wi, ffn1_wo, l)

        hl = _proj_in_call(x, m[3], m[4], nw[1], w_in, gain, l)
        ctx_cols = (NA_WIDTH, 3 * NA_WIDTH) if last else (0, IN_WIDTH)
        hc = _proj_in_call(xc, mc[3], mc[4], nw[1], w_in, gain, l, col_range=ctx_cols)
        hc = hc.reshape(b_, lc, ctx_cols[1] - ctx_cols[0])

        na = _na_call(hl, hc, (NA_WIDTH - ctx_cols[0]) // HEAD_DIM, bt, l, plan)
        fo = _fourier_call(hl, tables_lat, w_four, l)
        x = _proj_out_call(x, m[5], na, fo, w_out, l)

        if not last:
            nac = _ctx_attn_call(hc).reshape(1, b_ * lc, NA_WIDTH)
            foc = _fourier_call(hc, tables_ctx, w_four, l).reshape(1, b_ * lc, FOUR_WIDTH)
            xc = _proj_out_call(xc, mc[5], nac, foc, w_out, l)
            xc = _ffn_call(xc, mc[6], mc[7], mc[8], nw[2], ffn2_wi, ffn2_wo, l)

        x = _ffn_call(x, m[6], m[7], m[8], nw[2], ffn2_wi, ffn2_wo, l)
    return x
```

```python
import functools
import math

import jax
import jax.numpy as jnp
from jax import lax
from jax.experimental import pallas as pl
from jax.experimental.pallas import tpu as pltpu

GRID_W = 64
HEAD_DIM = 128
N_NA_HEADS = 12
NA_WIDTH = N_NA_HEADS * HEAD_DIM
N_FOUR_GROUPS = 4
FOUR_DIM = 128
FOUR_WIDTH = N_FOUR_GROUPS * FOUR_DIM
IN_WIDTH = 3 * NA_WIDTH + FOUR_WIDTH
WIN_ROWS = 8
WIN_COLS = 16
N_MOD = 9
EPS = 1e-6
LOG2E = math.log2(math.e)

BF16 = jnp.bfloat16
F32 = jnp.float32

V7X_VMEM_LIMIT_BYTES = 56 * 1024 * 1024
MOD_ROWS = 8
RMS_CHUNK_ROWS = 64
LANES = 128
DFT_ROW_BLOCK = 64
NA_GROUP_ROWS = 4
NA_HEADS_PER_STEP = 4
NA_SCORE_LOOKAHEAD = 2
NA_UNION_ROWS = 12


def _params(*sem):
    return pltpu.CompilerParams(dimension_semantics=sem, vmem_limit_bytes=V7X_VMEM_LIMIT_BYTES)


def _dot(a, b):
    return jnp.dot(a, b, preferred_element_type=F32)


def _dot_nt(a, b):
    return lax.dot_general(a, b, (((1,), (1,)), ((), ())), preferred_element_type=F32)


def _silu(x):
    return x / (1.0 + jnp.exp(-x))


def _rms_modulate_rows(x_ref, nw_ref, sh_ref, sc_ref, h_ref, rs_ref, copy_ref=None):
    n_chunks = x_ref.shape[0] // RMS_CHUNK_ROWS

    def rows_of(c):
        return pl.ds(pl.multiple_of(c * RMS_CHUNK_ROWS, RMS_CHUNK_ROWS), RMS_CHUNK_ROWS)

    def row_scale(c, carry):
        x = x_ref[rows_of(c), :]
        rs = lax.rsqrt(jnp.mean(x * x, axis=-1, keepdims=True) + EPS)
        rs_ref[rows_of(c), :] = jnp.broadcast_to(rs, (RMS_CHUNK_ROWS, rs_ref.shape[1]))
        return carry

    lax.fori_loop(0, n_chunks, row_scale, 0, unroll=4)

    def scale_rows(c, carry):
        rs = rs_ref[rows_of(c), :]
        for c0 in range(0, x_ref.shape[1], LANES):
            cols = slice(c0, c0 + LANES)
            x = x_ref[rows_of(c), cols]
            y = ((x * rs) * nw_ref[:, cols]) * (1.0 + sc_ref[:, cols]) + sh_ref[:, cols]
            h_ref[rows_of(c), cols] = y.astype(BF16)
            if copy_ref is not None:
                copy_ref[rows_of(c), cols] = x
        return carry

    lax.fori_loop(0, n_chunks, scale_rows, 0, unroll=2)


def _mod_body(c_ref, w_ref, b_ref, o_ref):
    s = _silu(c_ref[...]).astype(BF16)
    o_ref[...] = _dot(s, w_ref[...].astype(BF16)) + b_ref[...]


def _mod_call(cvec, w_mod, b_mod, tn=2048):
    depth, d, nout = w_mod.shape
    return pl.pallas_call(
        _mod_body,
        grid=(depth, nout // tn),
        in_specs=[
            pl.BlockSpec((MOD_ROWS, d), lambda l, j: (0, 0)),
            pl.BlockSpec((None, d, tn), lambda l, j: (l, 0, j)),
            pl.BlockSpec((None, 1, tn), lambda l, j: (l, 0, j)),
        ],
        out_specs=pl.BlockSpec((None, MOD_ROWS, tn), lambda l, j: (l, 0, j)),
        out_shape=jax.ShapeDtypeStruct((depth, MOD_ROWS, nout), F32),
        compiler_params=_params("arbitrary", "arbitrary"),
        name="mod",
    )(cvec, w_mod, b_mod.reshape(depth, 1, nout))


def _ffn_body(x_ref, sh_ref, sc_ref, g_ref, nw_ref, wi_hbm, wo_hbm, o_ref,
              h_scr, rs_scr, wg_buf, wu_buf, wo_buf, sem, *, layer, tf, nf, grid):
    tile = pl.program_id(0) * grid[1] + pl.program_id(1)
    n_tiles = grid[0] * grid[1]
    ff = nf * tf

    def weight_copies(f, slot):
        c0 = pl.multiple_of(f * tf, tf)
        c1 = pl.multiple_of(ff + f * tf, tf)
        return (
            pltpu.make_async_copy(wi_hbm.at[layer, :, pl.ds(c0, tf)], wg_buf.at[slot], sem.at[0, slot]),
            pltpu.make_async_copy(wi_hbm.at[layer, :, pl.ds(c1, tf)], wu_buf.at[slot], sem.at[1, slot]),
            pltpu.make_async_copy(wo_hbm.at[layer, pl.ds(c0, tf), :], wo_buf.at[slot], sem.at[2, slot]),
        )

    @pl.when(tile == 0)
    def _():
        for cp in weight_copies(0, 0):
            cp.start()

    _rms_modulate_rows(x_ref, nw_ref, sh_ref, sc_ref, h_scr, rs_scr, copy_ref=o_ref)
    gate = 0.5 * g_ref[...]

    def sub_step(f, slot, f_next, has_next):
        for cp in weight_copies(f, slot):
            cp.wait()

        @pl.when(has_next)
        def _():
            for cp in weight_copies(f_next, 1 - slot):
                cp.start()

        h = h_scr[...]
        g = _dot(h, wg_buf[slot].astype(BF16))
        u = _dot(h, wu_buf[slot].astype(BF16))
        a = (_silu(g) * u).astype(BF16)
        o_ref[...] += gate * _dot(a, wo_buf[slot].astype(BF16))

    def pair(p, carry):
        f0 = 2 * p
        sub_step(f0, 0, f0 + 1, True)
        sub_step(f0 + 1, 1, f0 + 2, True)
        return carry

    lax.fori_loop(0, nf // 2 - 1, pair, 0)
    sub_step(nf - 2, 0, nf - 1, True)
    sub_step(nf - 1, 1, 0, tile + 1 < n_tiles)


def _ffn_call(x, shift, scale, gate, nw, wi, wo, layer, tm=1024, tf=256):
    g_, r_, d = x.shape
    ff = wo.shape[1]
    nf = ff // tf
    assert nf % 2 == 0 and nf * tf == ff
    vec = pl.BlockSpec((None, 1, d), lambda g, i: (g, 0, 0))
    grid = (g_, r_ // tm)
    return pl.pallas_call(
        functools.partial(_ffn_body, layer=layer, tf=tf, nf=nf, grid=grid),
        grid=grid,
        in_specs=[
            pl.BlockSpec((None, tm, d), lambda g, i: (g, i, 0)),
            vec, vec, vec,
            pl.BlockSpec((1, d), lambda g, i: (0, 0)),
            pl.BlockSpec(memory_space=pl.ANY),
            pl.BlockSpec(memory_space=pl.ANY),
        ],
        out_specs=pl.BlockSpec((None, tm, d), lambda g, i: (g, i, 0)),
        out_shape=jax.ShapeDtypeStruct(x.shape, F32),
        scratch_shapes=[
            pltpu.VMEM((tm, d), BF16), pltpu.VMEM((tm, LANES), F32),
            pltpu.VMEM((2, d, tf), F32), pltpu.VMEM((2, d, tf), F32), pltpu.VMEM((2, tf, d), F32),
            pltpu.SemaphoreType.DMA((3, 2)),
        ],
        compiler_params=_params("arbitrary", "arbitrary"),
        name="ffn",
    )(x, shift, scale, gate, nw, wi, wo)


def _proj_in_body(x_ref, sh_ref, sc_ref, nw_ref, w_hbm, gain_ref, o_ref, h_scr, rs_scr, w_buf, sem,
                  *, layer, tn, n_norm_tiles, col_tiles, grid):
    tile = pl.program_id(0) * grid[1] + pl.program_id(1)
    n_tiles = grid[0] * grid[1]
    n_lo, n_hi = col_tiles

    def w_copy(n):
        slot = (n - n_lo) % 2
        return pltpu.make_async_copy(w_hbm.at[layer, :, pl.ds(n * tn, tn)], w_buf.at[slot], sem.at[slot])

    @pl.when(tile == 0)
    def _():
        w_copy(n_lo).start()
        w_copy(n_lo + 1).start()

    _rms_modulate_rows(x_ref, nw_ref, sh_ref, sc_ref, h_scr, rs_scr)

    w_copy(n_lo).wait()
    for n in range(n_lo, n_hi):
        slot = (n - n_lo) % 2
        acc = _dot(h_scr[...], w_buf[slot].astype(BF16))

        if n + 1 < n_hi:
            w_copy(n + 1).wait()
        if n + 2 < n_hi:
            w_copy(n + 2).start()
        else:
            @pl.when(tile + 1 < n_tiles)
            def _():
                w_copy(n + 2 - n_hi + n_lo).start()

        if n < n_norm_tiles:
            for hh in range(tn // HEAD_DIM):
                sl = slice(hh * HEAD_DIM, (hh + 1) * HEAD_DIM)
                gsl = slice(n * tn + hh * HEAD_DIM, n * tn + (hh + 1) * HEAD_DIM)
                osl = slice((n - n_lo) * tn + hh * HEAD_DIM, (n - n_lo) * tn + (hh + 1) * HEAD_DIM)
                t = acc[:, sl]
                y = t * lax.rsqrt(jnp.mean(t * t, axis=-1, keepdims=True) + EPS)
                o_ref[:, osl] = (y * gain_ref[:, gsl]).astype(BF16)
        else:
            o_ref[:, (n - n_lo) * tn:(n - n_lo + 1) * tn] = acc.astype(BF16)


def _proj_in_call(x, shift, scale, nw, w_in, gain, layer, col_range=None, tm=1024, tn=512):
    g_, r_, d = x.shape
    win = w_in.shape[2]
    lo, hi = (0, win) if col_range is None else col_range
    assert lo % tn == 0 and hi % tn == 0 and ((hi - lo) // tn) % 2 == 0 and (2 * NA_WIDTH) % tn == 0
    nout = hi - lo
    grid = (g_, r_ // tm)
    vec = pl.BlockSpec((None, 1, d), lambda g, i: (g, 0, 0))
    body = functools.partial(_proj_in_body, layer=layer, tn=tn, n_norm_tiles=2 * NA_WIDTH // tn,
                             col_tiles=(lo // tn, hi // tn), grid=grid)
    return pl.pallas_call(
        body,
        grid=grid,
        in_specs=[
            pl.BlockSpec((None, tm, d), lambda g, i: (g, i, 0)),
            vec, vec,
            pl.BlockSpec((1, d), lambda g, i: (0, 0)),
            pl.BlockSpec(memory_space=pl.ANY),
            pl.BlockSpec((1, win), lambda g, i: (0, 0)),
        ],
        out_specs=pl.BlockSpec((None, tm, nout), lambda g, i: (g, i, 0)),
        out_shape=jax.ShapeDtypeStruct((g_, r_, nout), BF16),
        scratch_shapes=[pltpu.VMEM((tm, d), BF16), pltpu.VMEM((tm, LANES), F32),
                        pltpu.VMEM((2, d, tn), F32), pltpu.SemaphoreType.DMA((2,))],
        compiler_params=_params("arbitrary", "arbitrary"),
        name="proj_in",
    )(x, shift, scale, nw, w_in, gain)


def _na_plan(rows, kh):
    assert rows % NA_GROUP_ROWS == 0 and rows >= NA_UNION_ROWS and NA_UNION_ROWS % 2 == 0
    assert NA_UNION_ROWS >= NA_GROUP_ROWS + kh - 1
    groups, entries = [], {}
    for r0 in range(0, rows, NA_GROUP_ROWS):
        u0 = min(max(r0 - kh // 2, 0), rows - NA_UNION_ROWS)
        keys = []
        for r in range(r0, r0 + NA_GROUP_ROWS):
            rs = min(max(r - kh // 2, 0), rows - kh)
            assert u0 <= rs and rs + kh <= u0 + NA_UNION_ROWS
            row = []
            for j0 in range(u0, u0 + NA_UNION_ROWS, 2):
                v0, v1 = rs <= j0 < rs + kh, rs <= j0 + 1 < rs + kh
                if v0 and v1:
                    key = ("both", j0 - r + WIN_ROWS - 1)
                elif v0:
                    key = ("lo", j0 - r + WIN_ROWS - 1)
                elif v1:
                    key = ("hi", j0 + 1 - r + WIN_ROWS - 1)
                else:
                    key = None
                if key is not None:
                    entries.setdefault(key, len(entries))
                row.append(key)
            keys.append(row)
        groups.append((r0, u0, keys))
    return groups, entries


def _bias_body(pairs_ref, o_ref, *, entries):
    shape = (GRID_W, 2 * GRID_W)
    n_dc = 2 * WIN_COLS - 1
    cq = lax.broadcasted_iota(jnp.int32, shape, 0)
    lane = lax.broadcasted_iota(jnp.int32, shape, 1)
    ck = lane & (GRID_W - 1)
    hi = lane >= GRID_W
    dc = jnp.clip(ck - cq, -(WIN_COLS - 1), WIN_COLS - 1) + (WIN_COLS - 1)
    idx = dc + jnp.where(hi, n_dc, 0)
    col_start = jnp.clip(cq - WIN_COLS // 2, 0, GRID_W - WIN_COLS)
    col_in = (ck >= col_start) & (ck < col_start + WIN_COLS)
    valid = {"both": col_in, "hi": col_in & hi, "lo": col_in & ~hi}
    for h in range(o_ref.shape[0]):
        for (kind, t), slot in entries.items():
            row = t - 1 if kind == "hi" else t
            src = jnp.broadcast_to(pairs_ref[h, row:row + 1, :], shape)
            tile = jnp.take_along_axis(src, idx, axis=1)
            o_ref[h, slot] = jnp.where(valid[kind], tile * LOG2E, -jnp.inf)


def _bias_call(rpb, entries):
    depth, heads, n_dr, n_dc = rpb.shape
    assert 2 * n_dc <= 2 * GRID_W
    nxt = jnp.concatenate([rpb[:, :, 1:], jnp.zeros_like(rpb[:, :, :1])], axis=2)
    pairs = jnp.concatenate([rpb, nxt, jnp.zeros((depth, heads, n_dr, 2 * GRID_W - 2 * n_dc), F32)], axis=3)
    nslot = len(entries)
    return pl.pallas_call(
        functools.partial(_bias_body, entries=entries),
        grid=(depth,),
        in_specs=[pl.BlockSpec((None, heads, n_dr, 2 * GRID_W), lambda l: (l, 0, 0, 0))],
        out_specs=pl.BlockSpec((heads, nslot, GRID_W, 2 * GRID_W), lambda l: (l, 0, 0, 0)),
        out_shape=jax.ShapeDtypeStruct((depth * heads, nslot, GRID_W, 2 * GRID_W), F32),
        compiler_params=_params("arbitrary"),
        name="bias",
    )(pairs)


def _na_body(q_ref, k_ref, v_ref, kc_ref, vc_ref, bt_ref, o_ref, *, groups, entries):
    lc = kc_ref.shape[0]
    pair = 2 * GRID_W
    nwin = NA_UNION_ROWS * GRID_W

    def slices(group):
        r0, u0, _ = group
        return (slice(r0 * GRID_W, (r0 + NA_GROUP_ROWS) * GRID_W),
                slice(u0 * GRID_W, (u0 + NA_UNION_ROWS) * GRID_W))

    def scores(head, group):
        qs, us = slices(group)
        hs = slice(head * HEAD_DIM, (head + 1) * HEAD_DIM)
        q = q_ref[qs, hs]
        return _dot_nt(q, k_ref[us, hs]), _dot_nt(q, kc_ref[:, hs])

    def finish(head, group, s_w, s_c):
        qs, us = slices(group)
        hs = slice(head * HEAD_DIM, (head + 1) * HEAD_DIM)
        inv_l, p_rows = [], []
        for i, row in enumerate(group[2]):
            rsl = slice(i * GRID_W, (i + 1) * GRID_W)
            sc_i = s_c[rsl]
            tiles = {p: s_w[rsl, p * pair:(p + 1) * pair] + bt_ref[head, entries[key]]
                     for p, key in enumerate(row) if key is not None}
            m_el = sc_i[:, :pair]
            for c0 in range(pair, lc, pair):
                m_el = jnp.maximum(m_el, sc_i[:, c0:c0 + pair])
            for t in tiles.values():
                m_el = jnp.maximum(m_el, t)
            m = jnp.max(m_el, axis=-1, keepdims=True)
            e_c = jnp.exp2(sc_i - m)
            l_el = e_c[:, :pair]
            for c0 in range(pair, lc, pair):
                l_el = l_el + e_c[:, c0:c0 + pair]
            parts = []
            for p in range(len(row)):
                if p in tiles:
                    e = jnp.exp2(tiles[p] - m)
                    l_el = l_el + e
                    parts.append(e.astype(BF16))
                else:
                    parts.append(jnp.zeros((GRID_W, pair), BF16))
            parts.append(e_c.astype(BF16))
            p_rows.append(jnp.concatenate(parts, axis=1))
            inv_l.append(1.0 / jnp.sum(l_el, axis=-1, keepdims=True))
        return jnp.concatenate(p_rows, axis=0), jnp.concatenate(inv_l, axis=0)

    def weighted_values(head, group, p_all, inv_l):
        qs, us = slices(group)
        hs = slice(head * HEAD_DIM, (head + 1) * HEAD_DIM)
        o = _dot(p_all[:, :nwin], v_ref[us, hs]) + _dot(p_all[:, nwin:], vc_ref[:, hs])
        o_ref[qs, hs] = (o * inv_l).astype(BF16)

    work = [(head, group) for head in range(NA_HEADS_PER_STEP) for group in groups]
    pending = [scores(*w) for w in work[:NA_SCORE_LOOKAHEAD]]
    probs = None
    for wi, w in enumerate(work):
        if wi + NA_SCORE_LOOKAHEAD < len(work):
            pending.append(scores(*work[wi + NA_SCORE_LOOKAHEAD]))
        new_probs = (w, finish(*w, *pending.pop(0)))
        if probs is not None:
            weighted_values(*probs[0], *probs[1])
        probs = new_probs
    weighted_values(*probs[0], *probs[1])


def _na_call(hl, hc, ctx_k_block, bt, layer, plan):
    b_, n_, _ = hl.shape
    lc = hc.shape[1]
    groups, entries = plan
    nslot = len(entries)
    hps = NA_HEADS_PER_STEP
    assert N_NA_HEADS % hps == 0 and ctx_k_block % hps == 0
    kofs = N_NA_HEADS // hps
    cofs = ctx_k_block // hps
    width = hps * HEAD_DIM
    body = functools.partial(_na_body, groups=groups, entries=entries)
    return pl.pallas_call(
        body,
        grid=(b_, N_NA_HEADS // hps),
        in_specs=[
            pl.BlockSpec((None, n_, width), lambda b, h: (b, 0, h)),
            pl.BlockSpec((None, n_, width), lambda b, h: (b, 0, kofs + h)),
            pl.BlockSpec((None, n_, width), lambda b, h: (b, 0, 2 * kofs + h)),
            pl.BlockSpec((None, lc, width), lambda b, h: (b, 0, cofs + h)),
            pl.BlockSpec((None, lc, width), lambda b, h: (b, 0, cofs + kofs + h)),
            pl.BlockSpec((hps, nslot, GRID_W, 2 * GRID_W), lambda b, h: (layer * kofs + h, 0, 0, 0)),
        ],
        out_specs=pl.BlockSpec((None, n_, width), lambda b, h: (b, 0, h)),
        out_shape=jax.ShapeDtypeStruct((b_, n_, NA_WIDTH), BF16),
        compiler_params=_params("arbitrary", "arbitrary"),
        name="na",
    )(hl, hl, hl, hc, hc, bt)


def _ctx_attn_body(q_ref, k_ref, v_ref, o_ref):
    for h in range(N_NA_HEADS):
        sl = slice(h * HEAD_DIM, (h + 1) * HEAD_DIM)
        s = _dot_nt(q_ref[:, sl], k_ref[:, sl])
        m = jnp.max(s, axis=-1, keepdims=True)
        e = jnp.exp2(s - m)
        l = jnp.sum(e, axis=-1, keepdims=True)
        o_ref[:, sl] = (_dot(e.astype(BF16), v_ref[:, sl]) / l).astype(BF16)


def _ctx_attn_call(hc):
    b_, lc, _ = hc.shape
    return pl.pallas_call(
        _ctx_attn_body,
        grid=(b_,),
        in_specs=[
            pl.BlockSpec((None, lc, NA_WIDTH), lambda b: (b, 0, 0)),
            pl.BlockSpec((None, lc, NA_WIDTH), lambda b: (b, 0, 1)),
            pl.BlockSpec((None, lc, NA_WIDTH), lambda b: (b, 0, 2)),
        ],
        out_specs=pl.BlockSpec((None, lc, NA_WIDTH), lambda b: (b, 0, 0)),
        out_shape=jax.ShapeDtypeStruct((b_, lc, NA_WIDTH), BF16),
        compiler_params=_params("arbitrary"),
        name="ctx_attn",
    )(hc, hc, hc)


def _dft_tables(length):
    def cos_sin(freq, n_pos, period):
        ang = ((freq[:, None] * jnp.arange(n_pos, dtype=jnp.int32)[None, :]) % period).astype(F32)
        ang = ang * (2.0 * math.pi / period)
        return jnp.cos(ang), jnp.sin(ang)

    assert length % DFT_ROW_BLOCK == 0
    coarse = jnp.arange(length // DFT_ROW_BLOCK, dtype=jnp.int32) * DFT_ROW_BLOCK
    fine = jnp.arange(DFT_ROW_BLOCK, dtype=jnp.int32)
    ca, sa = cos_sin(coarse, length, length)
    cb, sb = cos_sin(fine, length, length)
    cc, sc = cos_sin(jnp.arange(FOUR_DIM, dtype=jnp.int32), FOUR_DIM, FOUR_DIM)
    scale = 1.0 / math.sqrt(length * FOUR_DIM)
    chan = jnp.concatenate([cc * scale, sc * scale], axis=1).astype(BF16)
    return ca, sa, cb, sb, chan


def _fourier_body(f_ref, chan_ref, ca_ref, sa_ref, cb_ref, sb_ref, wf_ref, o_ref, cl_scr, sl_scr, zc_scr, zs_scr):
    @pl.when(pl.program_id(0) == 0)
    def _():
        cb = cb_ref[...]
        sb = sb_ref[...]
        for a in range(ca_ref.shape[0]):
            rows = slice(a * DFT_ROW_BLOCK, (a + 1) * DFT_ROW_BLOCK)
            ca = ca_ref[a:a + 1, :]
            sa = sa_ref[a:a + 1, :]
            cl_scr[rows, :] = (ca * cb - sa * sb).astype(BF16)
            sl_scr[rows, :] = (-(sa * cb + ca * sb)).astype(BF16)

    for g in range(N_FOUR_GROUPS):
        sl = slice(g * FOUR_DIM, (g + 1) * FOUR_DIM)
        z = _dot(f_ref[:, sl], chan_ref[...])
        zc_scr[:, sl] = z[:, :FOUR_DIM].astype(BF16)
        zs_scr[:, sl] = z[:, FOUR_DIM:].astype(BF16)
    mixed = _dot(cl_scr[...], zc_scr[...]) + _dot(sl_scr[...], zs_scr[...])
    for g in range(N_FOUR_GROUPS):
        sl = slice(g * FOUR_DIM, (g + 1) * FOUR_DIM)
        o_ref[:, sl] = _dot(mixed[:, sl].astype(BF16), wf_ref[g].astype(BF16)).astype(BF16)


def _fourier_call(h, tables, w_four, layer):
    b_, length, _ = h.shape
    ca, sa, cb, sb, chan = tables
    fblk = 3 * NA_WIDTH // FOUR_WIDTH
    whole = lambda arr: pl.BlockSpec(arr.shape, lambda b: (0,) * arr.ndim)
    return pl.pallas_call(
        _fourier_body,
        grid=(b_,),
        in_specs=[
            pl.BlockSpec((None, length, FOUR_WIDTH), lambda b: (b, 0, fblk)),
            whole(chan), whole(ca), whole(sa), whole(cb), whole(sb),
            pl.BlockSpec((None, N_FOUR_GROUPS, FOUR_DIM, FOUR_DIM), lambda b: (layer, 0, 0, 0)),
        ],
        out_specs=pl.BlockSpec((None, length, FOUR_WIDTH), lambda b: (b, 0, 0)),
        out_shape=jax.ShapeDtypeStruct((b_, length, FOUR_WIDTH), BF16),
        scratch_shapes=[pltpu.VMEM((length, length), BF16), pltpu.VMEM((length, length), BF16),
                        pltpu.VMEM((length, FOUR_WIDTH), BF16), pltpu.VMEM((length, FOUR_WIDTH), BF16)],
        compiler_params=_params("arbitrary"),
        name="fourier",
    )(h, chan, ca, sa, cb, sb, w_four)


def _proj_out_body(x_ref, g_ref, na_ref, fo_ref, w_ref, o_ref, wb_scr):
    @pl.when((pl.program_id(0) == 0) & (pl.program_id(1) == 0))
    def _():
        wb_scr[...] = w_ref[...].astype(BF16)

    y = _dot(na_ref[...], wb_scr[:NA_WIDTH, :]) + _dot(fo_ref[...], wb_scr[NA_WIDTH:, :])
    o_ref[...] = x_ref[...] + g_ref[...] * y


def _proj_out_call(x, gate, na, fo, w_out, layer, tm=512):
    g_, r_, d = x.shape
    kin = w_out.shape[1]
    return pl.pallas_call(
        _proj_out_body,
        grid=(g_, r_ // tm),
        in_specs=[
            pl.BlockSpec((None, tm, d), lambda g, i: (g, i, 0)),
            pl.BlockSpec((None, 1, d), lambda g, i: (g, 0, 0)),
            pl.BlockSpec((None, tm, NA_WIDTH), lambda g, i: (g, i, 0)),
            pl.BlockSpec((None, tm, FOUR_WIDTH), lambda g, i: (g, i, 0)),
            pl.BlockSpec((None, kin, d), lambda g, i: (layer, 0, 0), pipeline_mode=pl.Buffered(1)),
        ],
        out_specs=pl.BlockSpec((None, tm, d), lambda g, i: (g, i, 0)),
        out_shape=jax.ShapeDtypeStruct(x.shape, F32),
        scratch_shapes=[pltpu.VMEM((kin, d), BF16)],
        compiler_params=_params("arbitrary", "arbitrary"),
        name="proj_out",
    )(x, gate, na, fo, w_out)


def kernel(x, c, ctx, c_ctx, w_mod, b_mod, norm_w, ffn1_wi, ffn1_wo, w_in, q_norm_w, k_norm_w, rpb,
           w_four, w_out, ffn2_wi, ffn2_wo):
    b_, n_, d = x.shape
    lc = ctx.shape[1]
    depth = w_mod.shape[0]
    rows = n_ // GRID_W

    cvec = jnp.zeros((MOD_ROWS, d), F32).at[:b_].set(c).at[b_].set(c_ctx)
    mod = _mod_call(cvec, w_mod, b_mod).reshape(depth, MOD_ROWS, N_MOD, d)

    plan = _na_plan(rows, min(WIN_ROWS, rows))
    bt = _bias_call(rpb, plan[1])
    tables_lat = _dft_tables(n_)
    tables_ctx = _dft_tables(lc)
    ones = jnp.ones((NA_WIDTH + FOUR_WIDTH,), F32)

    xc = ctx.reshape(1, b_ * lc, d)
    for l in range(depth):
        last = l == depth - 1
        m = [mod[l, :b_, i].reshape(b_, 1, d) for i in range(N_MOD)]
        mc = [mod[l, b_, i].reshape(1, 1, d) for i in range(N_MOD)]
        nw = [norm_w[l, i].reshape(1, d) for i in range(3)]
        gain = jnp.concatenate([
            jnp.tile(q_norm_w[l] * (HEAD_DIM ** -0.5 * LOG2E), N_NA_HEADS),
            jnp.tile(k_norm_w[l], N_NA_HEADS), ones]).reshape(1, IN_WIDTH)

        x = _ffn_call(x, m[0], m[1], m[2], nw[0], ffn1_wi, ffn1_wo, l)
        xc = _ffn_call(xc, mc[0], mc[1], mc[2], nw[0], ffn1_wi, ffn1_wo, l)

        hl = _proj_in_call(x, m[3], m[4], nw[1], w_in, gain, l)
        ctx_cols = (NA_WIDTH, 3 * NA_WIDTH) if last else (0, IN_WIDTH)
        hc = _proj_in_call(xc, mc[3], mc[4], nw[1], w_in, gain, l, col_range=ctx_cols)
        hc = hc.reshape(b_, lc, ctx_cols[1] - ctx_cols[0])

        na = _na_call(hl, hc, (NA_WIDTH - ctx_cols[0]) // HEAD_DIM, bt, l, plan)
        fo = _fourier_call(hl, tables_lat, w_four, l)
        x = _proj_out_call(x, m[5], na, fo, w_out, l)

        if not last:
            nac = _ctx_attn_call(hc).reshape(1, b_ * lc, NA_WIDTH)
            foc = _fourier_call(hc, tables_ctx, w_four, l).reshape(1, b_ * lc, FOUR_WIDTH)
            xc = _proj_out_call(xc, mc[5], nac, foc, w_out, l)
            xc = _ffn_call(xc, mc[6], mc[7], mc[8], nw[2], ffn2_wi, ffn2_wo, l)

        x = _ffn_call(x, m[6], m[7], m[8], nw[2], ffn2_wi, ffn2_wo, l)
    return x
```
